```python
import jax, jax.numpy as jnp
from jax import lax
import numpy as np

D_MODEL = 2048
BATCH = 4
SEQ = 8192
DEPTH = 2

CHUNK = 64

D_ATTN = D_MODEL // 2
HEAD_DIM = 128
N_FOX_HEADS = D_ATTN // HEAD_DIM
D_POOL = D_MODEL - D_ATTN
POOL_WINDOWS = (2, 4, 8, 16)
N_POOL_GROUPS = len(POOL_WINDOWS)
POOL_GROUP_DIM = D_POOL // N_POOL_GROUPS
D_PROJ = 3 * D_ATTN + N_FOX_HEADS + D_POOL
Q_BLOCK = 128

N_EXPERTS = 64
TOP_K = 8
N_EXPERT_GROUPS = 8
TOPK_GROUPS = 4
EXPERTS_PER_GROUP = N_EXPERTS // N_EXPERT_GROUPS
D_EXPERT = D_MODEL // 4
ROUTED_SCALE = 2.5
EXPERT_BLOCK = 512

N_MOD = 6
EPS = 1e-6

kernel_name = "hybrid_fox_pool_moe_adaln"


def rmsnorm(x, g):
    xf = x.astype(jnp.float32)
    y = xf * lax.rsqrt(jnp.mean(xf * xf, axis=-1, keepdims=True) + EPS)
    return (y * g.astype(jnp.float32)).astype(x.dtype)


def modulate(h, shift, scale):
    return h * (1 + scale) + shift


def forgetting_attention(q, k, v, log_f):
    Bb, S, H, Dh = q.shape
    dcum = jnp.transpose(jnp.cumsum(log_f, axis=1), (0, 2, 1))
    kpos = jnp.arange(S)
    scale = Dh ** -0.5

    def block(i):
        start = i * Q_BLOCK
        qb = lax.dynamic_slice_in_dim(q, start, Q_BLOCK, axis=1)
        dq = lax.dynamic_slice_in_dim(dcum, start, Q_BLOCK, axis=2)
        logits = jnp.einsum('bqhd,bkhd->bhqk', qb, k).astype(jnp.float32) * scale
        logits = logits + dq[..., :, None] - dcum[:, :, None, :]
        qpos = start + jnp.arange(Q_BLOCK)
        logits = jnp.where(kpos[None, :] <= qpos[:, None], logits, -jnp.inf)
        p = jax.nn.softmax(logits, axis=-1).astype(v.dtype)
        return jnp.einsum('bhqk,bkhd->bqhd', p, v)

    out = lax.map(block, jnp.arange(S // Q_BLOCK))
    return jnp.transpose(out, (1, 0, 2, 3, 4)).reshape(Bb, S, H * Dh)


def multiscale_pool(u, w_pool, pool_scale):
    Bb, S, C = u.shape
    gc = POOL_GROUP_DIM
    uf = u.astype(jnp.float32)
    csum = jnp.concatenate([jnp.zeros((Bb, 1, C), jnp.float32), jnp.cumsum(uf, axis=1)], axis=1)
    outs = []
    for g, w in enumerate(POOL_WINDOWS):
        cg = csum[..., g * gc:(g + 1) * gc]
        lagged = jnp.concatenate([jnp.zeros((Bb, w - 1, gc), jnp.float32), cg[:, :S - w + 1]], axis=1)
        cnt = jnp.minimum(jnp.arange(1, S + 1), w).astype(jnp.float32)[None, :, None]
        outs.append((cg[:, 1:] - lagged) / cnt - uf[..., g * gc:(g + 1) * gc])
    pooled = jnp.stack(outs, axis=2).astype(u.dtype)
    y = jnp.einsum('bsgc,gcd->bsgd', pooled, w_pool).reshape(Bb, S, C)
    return y * pool_scale


def swiglu(x, w1, w3, w2):
    return (jax.nn.silu(x @ w1) * (x @ w3)) @ w2


def moe_ffn(h, w_router, router_bias, w1, w3, w2, w1_s, w3_s, w2_s):
    Bb, S, D = h.shape
    T = Bb * S
    xt = h.reshape(T, D)
    scores = jax.nn.sigmoid((xt @ w_router).astype(jnp.float32))
    biased = scores + router_bias.astype(jnp.float32)
    grp = biased.reshape(T, N_EXPERT_GROUPS, EXPERTS_PER_GROUP)
    grp_score = lax.top_k(grp, 2)[0].sum(-1)
    _, top_g = lax.top_k(grp_score, TOPK_GROUPS)
    gmask = jnp.any(top_g[..., None] == jnp.arange(N_EXPERT_GROUPS), axis=1)
    emask = jnp.repeat(gmask, EXPERTS_PER_GROUP, axis=1)
    _, idx = lax.top_k(jnp.where(emask, biased, -jnp.inf), TOP_K)
    wts = jnp.take_along_axis(scores, idx, axis=-1)
    wts = wts / jnp.sum(wts, axis=-1, keepdims=True) * ROUTED_SCALE

    A = T * TOP_K
    flat_e = idx.reshape(A).astype(jnp.int32)
    flat_tok = jnp.arange(A, dtype=jnp.int32) // TOP_K
    flat_w = wts.reshape(A)
    order = jnp.argsort(flat_e)
    se = flat_e[order]
    counts = jnp.bincount(flat_e, length=N_EXPERTS).astype(jnp.int32)
    padded = ((counts + EXPERT_BLOCK - 1) // EXPERT_BLOCK) * EXPERT_BLOCK
    start = jnp.cumsum(counts) - counts
    cum_padded = jnp.cumsum(padded)
    pstart = cum_padded - padded
    dest = pstart[se] + jnp.arange(A, dtype=jnp.int32) - start[se]
    n_blocks = -(-A // EXPERT_BLOCK) + N_EXPERTS
    P = n_blocks * EXPERT_BLOCK
    buf_tok = jnp.full((P,), T, jnp.int32).at[dest].set(flat_tok[order])
    buf_w = jnp.zeros((P,), jnp.float32).at[dest].set(flat_w[order])
    block_starts = jnp.arange(n_blocks, dtype=jnp.int32) * EXPERT_BLOCK
    block_exp = jnp.minimum(jnp.searchsorted(cum_padded, block_starts, side='right'), N_EXPERTS - 1)
    xpad = jnp.concatenate([xt, jnp.zeros((1, D), xt.dtype)], axis=0)

    def body(out, blk):
        tok, w, e = blk
        xb = xpad[tok]
        yb = swiglu(xb, w1[e], w3[e], w2[e]).astype(jnp.float32) * w[:, None]
        return out.at[tok].add(yb), None

    routed, _ = lax.scan(body, jnp.zeros((T + 1, D), jnp.float32),
                         (buf_tok.reshape(n_blocks, EXPERT_BLOCK), buf_w.reshape(n_blocks, EXPERT_BLOCK), block_exp))
    y = routed[:T].astype(h.dtype) + swiglu(xt, w1_s, w3_s, w2_s)
    return y.reshape(Bb, S, D)


def setup_inputs(seed: int = 0) -> dict:
    key = jax.random.key(seed)
    ks = jax.random.split(key, 24)
    f32 = jnp.float32
    nrm = lambda k, shape, s: jax.random.normal(k, shape, f32) * s
    D = D_MODEL
    return {
        "x": nrm(ks[0], (BATCH, SEQ, D), 1.0),
        "c": nrm(ks[1], (BATCH, D), 1.0),
        "norm1_g": 1.0 + nrm(ks[2], (DEPTH, D), 0.02),
        "norm2_g": 1.0 + nrm(ks[3], (DEPTH, D), 0.02),
        "w_ada": nrm(ks[4], (DEPTH, D, N_MOD * D), 0.3 * D ** -0.5),
        "b_ada": nrm(ks[5], (DEPTH, N_MOD * D), 0.02),
        "w_in": nrm(ks[6], (DEPTH, D, D_PROJ), D ** -0.5),
        "b_f": jax.random.uniform(ks[7], (DEPTH, N_FOX_HEADS), f32, 1.0, 6.0),
        "w_pool": nrm(ks[8], (DEPTH, N_POOL_GROUPS, POOL_GROUP_DIM, POOL_GROUP_DIM), POOL_GROUP_DIM ** -0.5),
        "pool_scale": 1.0 + nrm(ks[9], (DEPTH, D_POOL), 0.1),
        "w_o": nrm(ks[10], (DEPTH, D, D), D ** -0.5),
        "w_router": nrm(ks[11], (DEPTH, D, N_EXPERTS), D ** -0.5),
        "router_bias": nrm(ks[12], (DEPTH, N_EXPERTS), 0.01),
        "w1": nrm(ks[13], (DEPTH, N_EXPERTS, D, D_EXPERT), D ** -0.5),
        "w3": nrm(ks[14], (DEPTH, N_EXPERTS, D, D_EXPERT), D ** -0.5),
        "w2": nrm(ks[15], (DEPTH, N_EXPERTS, D_EXPERT, D), D_EXPERT ** -0.5),
        "w1_shared": nrm(ks[16], (DEPTH, D, D_EXPERT), D ** -0.5),
        "w3_shared": nrm(ks[17], (DEPTH, D, D_EXPERT), D ** -0.5),
        "w2_shared": nrm(ks[18], (DEPTH, D_EXPERT, D), D_EXPERT ** -0.5),
        "final_g": 1.0 + nrm(ks[19], (D,), 0.02),
    }


def reference(x, c, norm1_g, norm2_g, w_ada, b_ada, w_in, b_f, w_pool, pool_scale, w_o,
              w_router, router_bias, w1, w3, w2, w1_shared, w3_shared, w2_shared, final_g):
    Bb, S, D = x.shape
    c_act = jax.nn.silu(c)
    q_end, k_end, v_end, f_end = D_ATTN, 2 * D_ATTN, 3 * D_ATTN, 3 * D_ATTN + N_FOX_HEADS
    for l in range(DEPTH):
        mod = (c_act @ w_ada[l] + b_ada[l])[:, None, :]
        sh1, sc1, g1, sh2, sc2, g2 = jnp.split(mod, N_MOD, axis=-1)

        h = modulate(rmsnorm(x, norm1_g[l]), sh1, sc1)
        proj = h @ w_in[l]
        q = proj[..., :q_end].reshape(Bb, S, N_FOX_HEADS, HEAD_DIM)
        k = proj[..., q_end:k_end].reshape(Bb, S, N_FOX_HEADS, HEAD_DIM)
        v = proj[..., k_end:v_end].reshape(Bb, S, N_FOX_HEADS, HEAD_DIM)
        log_f = jax.nn.log_sigmoid((proj[..., v_end:f_end] + b_f[l]).astype(jnp.float32))
        attn = forgetting_attention(q, k, v, log_f)
        pool = multiscale_pool(proj[..., f_end:], w_pool[l], pool_scale[l])
        mix = jnp.concatenate([attn, pool], axis=-1) @ w_o[l]
        x = x + g1 * mix

        h = modulate(rmsnorm(x, norm2_g[l]), sh2, sc2)
        x = x + g2 * moe_ffn(h, w_router[l], router_bias[l], w1[l], w3[l], w2[l],
                             w1_shared[l], w3_shared[l], w2_shared[l])
    return rmsnorm(x, final_g)
```

```python
import functools

import jax
import jax.numpy as jnp
from jax import lax
from jax.experimental import pallas as pl
from jax.experimental.pallas import tpu as pltpu

F32 = jnp.float32
BF16 = jnp.bfloat16
I32 = jnp.int32

EPS = 1e-6
HEAD_DIM = 128
LANES = 128
POOL_WINDOWS = (2, 4, 8, 16)
POOL_HALO = 16
TOP_K = 8
N_EXPERT_GROUPS = 8
TOPK_GROUPS = 4
ROUTED_SCALE = 2.5
EXPERT_BLOCK = 512
N_MOD = 6
VMEM_LIMIT = 56 * 1024 * 1024

_NT = (((1,), (1,)), ((), ()))


def _cparams(sem):
    return pltpu.CompilerParams(dimension_semantics=sem, vmem_limit_bytes=VMEM_LIMIT)


def _resident(shape, index_map):
    return pl.BlockSpec(shape, index_map, pipeline_mode=pl.Buffered(1))


def _ada_kernel(c_ref, w_ref, b_ref, o_ref):
    c = c_ref[...]
    ca = c * jax.nn.sigmoid(c)
    o_ref[0] = jnp.dot(ca, w_ref[0], precision=lax.Precision.HIGHEST,
                       preferred_element_type=F32) + b_ref[0]


def _ada(c, w_ada, b_ada):
    depth, d, n = w_ada.shape
    bsz = c.shape[0]
    rows = 8
    cp = jnp.zeros((rows, d), F32).at[:bsz].set(c)
    tn = 512
    out = pl.pallas_call(
        _ada_kernel,
        out_shape=jax.ShapeDtypeStruct((depth, rows, n), F32),
        grid=(depth, n // tn),
        in_specs=[pl.BlockSpec((rows, d), lambda l, j: (0, 0)),
                  pl.BlockSpec((1, d, tn), lambda l, j: (l, 0, j)),
                  pl.BlockSpec((1, 1, tn), lambda l, j: (l, 0, j))],
        out_specs=pl.BlockSpec((1, rows, tn), lambda l, j: (l, 0, j)),
        compiler_params=_cparams(("arbitrary", "arbitrary")),
        name="ada",
    )(cp, w_ada, b_ada.reshape(depth, 1, n))
    return out[:, :bsz]


def _proj_kernel(x_ref, g_ref, sh_ref, sc_ref, wq_ref, wk_ref, wv_ref, wf_ref, bf_ref,
                 wu_ref, wp_ref, ps_ref,
                 q_ref, k_ref, v_ref, dcol_ref, drow_ref, p_ref,
                 ubuf, dcarry, *, tm, n_heads, gc):
    i = pl.program_id(1)
    x = x_ref[0]
    ms = jnp.mean(x * x, axis=-1, keepdims=True)
    y = (x * lax.rsqrt(ms + EPS)) * g_ref[...]
    h = y * (1.0 + sc_ref[0]) + sh_ref[0]
    hb = h.astype(BF16)

    scale = HEAD_DIM ** -0.5
    q_ref[0] = (jnp.dot(hb, wq_ref[...], preferred_element_type=F32) * scale).astype(BF16)
    k_ref[0] = jnp.dot(hb, wk_ref[...], preferred_element_type=F32).astype(BF16)
    v_ref[0] = jnp.dot(hb, wv_ref[...], preferred_element_type=F32).astype(BF16)

    z = jnp.dot(hb, wf_ref[...], preferred_element_type=F32) + bf_ref[...]
    lf = jnp.minimum(z, 0.0) - jnp.log1p(jnp.exp(-jnp.abs(z)))
    r = lax.broadcasted_iota(I32, (tm, tm), 0)
    cidx = lax.broadcasted_iota(I32, (tm, tm), 1)
    tri = (cidx <= r).astype(F32)

    @pl.when(i == 0)
    def _():
        dcarry[...] = jnp.zeros_like(dcarry)
        ubuf[0:POOL_HALO, :] = jnp.zeros((POOL_HALO, ubuf.shape[1]), F32)

    @pl.when(i > 0)
    def _():
        ubuf[0:POOL_HALO, :] = ubuf[tm:tm + POOL_HALO, :]

    dc = jnp.dot(tri, lf, precision=lax.Precision.HIGHEST,
                 preferred_element_type=F32) + dcarry[...]
    dcarry[...] = dc[tm - 1:tm, :]
    dcol_ref[0] = dc
    drow_ref[0] = dc.T[0:drow_ref.shape[1], :]

    u = jnp.dot(hb, wu_ref[...], preferred_element_type=F32)
    ubuf[POOL_HALO:POOL_HALO + tm, :] = u
    pos = i * tm + lax.broadcasted_iota(I32, (tm, 1), 0)
    for g, w in enumerate(POOL_WINDOWS):
        s = ubuf[:, g * gc:(g + 1) * gc]
        step = 1
        while step < w:
            s = s + pltpu.roll(s, step, 0)
            step *= 2
        cnt = jnp.minimum(pos + 1, w).astype(F32)
        pooled = s[POOL_HALO:POOL_HALO + tm, :] / cnt - u[:, g * gc:(g + 1) * gc]
        yg = jnp.dot(pooled.astype(BF16), wp_ref[g], preferred_element_type=F32)
        p_ref[0, :, g * gc:(g + 1) * gc] = (yg * ps_ref[:, g * gc:(g + 1) * gc]).astype(BF16)


def _proj(x, g, sh, sc, wq, wk, wv, wf, bf, wu, wp, ps, *, n_heads):
    bsz, seq, d = x.shape
    da = wq.shape[1]
    dp = wu.shape[1]
    gc = dp // len(POOL_WINDOWS)
    tm = min(512, seq)
    nt = seq // tm
    kern = functools.partial(_proj_kernel, tm=tm, n_heads=n_heads, gc=gc)
    row = lambda b, i: (b, i, 0)
    per_b = lambda b, i: (b, 0, 0)
    const2 = lambda b, i: (0, 0)
    const3 = lambda b, i: (0, 0, 0)
    return pl.pallas_call(
        kern,
        out_shape=(jax.ShapeDtypeStruct((bsz, seq, da), BF16),
                   jax.ShapeDtypeStruct((bsz, seq, da), BF16),
                   jax.ShapeDtypeStruct((bsz, seq, da), BF16),
                   jax.ShapeDtypeStruct((bsz, seq, LANES), F32),
                   jax.ShapeDtypeStruct((bsz, 8, seq), F32),
                   jax.ShapeDtypeStruct((bsz, seq, dp), BF16)),
        grid=(bsz, nt),
        in_specs=[pl.BlockSpec((1, tm, d), row),
                  pl.BlockSpec((1, d), const2),
                  pl.BlockSpec((1, 1, d), per_b),
                  pl.BlockSpec((1, 1, d), per_b),
                  _resident((d, da), const2),
                  _resident((d, da), const2),
                  _resident((d, da), const2),
                  _resident((d, LANES), const2),
                  pl.BlockSpec((1, LANES), const2),
                  _resident((d, dp), const2),
                  _resident(wp.shape, const3),
                  pl.BlockSpec((1, dp), const2)],
        out_specs=(pl.BlockSpec((1, tm, da), row),
                   pl.BlockSpec((1, tm, da), row),
                   pl.BlockSpec((1, tm, da), row),
                   pl.BlockSpec((1, tm, LANES), row),
                   pl.BlockSpec((1, 8, tm), lambda b, i: (b, 0, i)),
                   pl.BlockSpec((1, tm, dp), row)),
        scratch_shapes=[pltpu.VMEM((tm + POOL_HALO, dp), F32),
                        pltpu.VMEM((1, LANES), F32)],
        compiler_params=_cparams(("arbitrary", "arbitrary")),
        name="proj",
    )(x, g, sh, sc, wq, wk, wv, wf, bf, wu, wp, ps)


def _attn_kernel(q_ref, k_ref, v_ref, dcol_ref, drow_ref, o_ref, *, tq):
    hd = pl.program_id(1)
    i = pl.program_id(2)
    q = q_ref[0]
    lane = lax.broadcasted_iota(I32, (tq, LANES), 1)
    dq = jnp.sum(jnp.where(lane == hd, dcol_ref[0], 0.0), axis=-1, keepdims=True)

    def block(j, carry, masked):
        m, l, acc = carry
        start = pl.multiple_of(j * tq, tq)
        ks = k_ref[0, pl.ds(start, tq), :]
        vs = v_ref[0, pl.ds(start, tq), :]
        dk = drow_ref[0, pl.ds(hd, 1), pl.ds(start, tq)]
        z = lax.dot_general(q, ks, _NT, preferred_element_type=F32) - dk
        if masked:
            rr = lax.broadcasted_iota(I32, (tq, tq), 0)
            cc = lax.broadcasted_iota(I32, (tq, tq), 1)
            z = jnp.where(cc <= rr, z, -jnp.inf)
        m_new = jnp.maximum(m, jnp.max(z, axis=-1, keepdims=True) + dq)
        p = jnp.exp(z - (m_new - dq))
        alpha = jnp.exp(m - m_new)
        l = alpha * l + jnp.sum(p, axis=-1, keepdims=True)
        acc = alpha * acc + jnp.dot(p.astype(BF16), vs, preferred_element_type=F32)
        return m_new, l, acc

    init = (jnp.full((tq, 1), -jnp.inf, F32), jnp.zeros((tq, 1), F32),
            jnp.zeros((tq, HEAD_DIM), F32))
    carry = lax.fori_loop(0, i, functools.partial(block, masked=False), init)
    _, l, acc = block(i, carry, True)
    o_ref[0] = (acc / l).astype(o_ref.dtype)


def _attn(q, k, v, dcol, drow, *, n_heads):
    bsz, seq, da = q.shape
    tq = min(512, seq)
    nq = seq // tq
    kern = functools.partial(_attn_kernel, tq=tq)
    return pl.pallas_call(
        kern,
        out_shape=jax.ShapeDtypeStruct((bsz, seq, da), BF16),
        grid=(bsz, n_heads, nq),
        in_specs=[pl.BlockSpec((1, tq, HEAD_DIM), lambda b, h, i: (b, i, h)),
                  pl.BlockSpec((1, seq, HEAD_DIM), lambda b, h, i: (b, 0, h)),
                  pl.BlockSpec((1, seq, HEAD_DIM), lambda b, h, i: (b, 0, h)),
                  pl.BlockSpec((1, tq, LANES), lambda b, h, i: (b, i, 0)),
                  pl.BlockSpec((1, 8, seq), lambda b, h, i: (b, 0, 0))],
        out_specs=pl.BlockSpec((1, tq, HEAD_DIM), lambda b, h, i: (b, i, h)),
        compiler_params=_cparams(("arbitrary", "arbitrary", "arbitrary")),
        name="attn",
    )(q, k, v, dcol, drow)


def _post_kernel(a_ref, p_ref, x_ref, g1_ref, ng_ref, sh_ref, sc_ref, woa_ref, wop_ref,
                 wr1_ref, wr2_ref, rb_ref,
                 x1_ref, h2_ref, idx_ref, wts_ref, rank_ref, cnt_ref,
                 carry, *, tm, n_exp):
    first = jnp.logical_and(pl.program_id(0) == 0, pl.program_id(1) == 0)

    @pl.when(first)
    def _():
        carry[...] = jnp.zeros_like(carry)

    mix = jnp.dot(a_ref[0], woa_ref[...], preferred_element_type=F32)
    mix = mix + jnp.dot(p_ref[0], wop_ref[...], preferred_element_type=F32)
    x1 = x_ref[0] + g1_ref[0] * mix
    x1_ref[0] = x1
    ms = jnp.mean(x1 * x1, axis=-1, keepdims=True)
    y = (x1 * lax.rsqrt(ms + EPS)) * ng_ref[...]
    h2 = y * (1.0 + sc_ref[0]) + sh_ref[0]
    h2_ref[0] = h2

    h_hi = h2.astype(BF16)
    h_lo = (h2 - h_hi.astype(F32)).astype(BF16)
    a1 = jnp.dot(h_hi, wr1_ref[...], preferred_element_type=F32)
    a2 = jnp.dot(h_lo, wr2_ref[...], preferred_element_type=F32)
    logits = a1[:, :LANES] + a1[:, LANES:] + a2
    lt = logits.T[0:n_exp, :]
    sc = jax.nn.sigmoid(lt)
    biased = sc + rb_ref[...]

    epg = n_exp // N_EXPERT_GROUPS
    sub = lax.broadcasted_iota(I32, (epg, tm), 0)
    blocks, gscore = [], []
    for g in range(N_EXPERT_GROUPS):
        blk = biased[g * epg:(g + 1) * epg, :]
        m1 = jnp.max(blk, axis=0, keepdims=True)
        f1 = jnp.min(jnp.where(blk == m1, sub, epg), axis=0, keepdims=True)
        m2 = jnp.max(jnp.where(sub == f1, -jnp.inf, blk), axis=0, keepdims=True)
        blocks.append(blk)
        gscore.append(m1 + m2)
    masked = []
    for g in range(N_EXPERT_GROUPS):
        beats = jnp.zeros((1, tm), I32)
        for g2 in range(N_EXPERT_GROUPS):
            if g2 == g:
                continue
            win = (gscore[g2] > gscore[g]) if g2 > g else (gscore[g2] >= gscore[g])
            beats = beats + win.astype(I32)
        masked.append(jnp.where(beats < TOPK_GROUPS, blocks[g], -jnp.inf))
    v = jnp.concatenate(masked, axis=0)

    eio = lax.broadcasted_iota(I32, (n_exp, tm), 0)
    base = carry[...]
    hits, svals = [], []
    sel = jnp.zeros((n_exp, tm), F32)
    for k in range(TOP_K):
        m = jnp.max(v, axis=0, keepdims=True)
        idx = jnp.min(jnp.where(v == m, eio, n_exp), axis=0, keepdims=True)
        hit = eio == idx
        svals.append(jnp.sum(jnp.where(hit, sc, 0.0), axis=0, keepdims=True))
        v = jnp.where(hit, -jnp.inf, v)
        sel = sel + hit.astype(F32)
        hits.append(hit)
        idx_ref[k:k + 1, :] = idx
    total = svals[0]
    for k in range(1, TOP_K):
        total = total + svals[k]
    for k in range(TOP_K):
        wts_ref[k:k + 1, :] = svals[k] / total * ROUTED_SCALE

    rr = lax.broadcasted_iota(I32, (tm, tm), 0)
    cc = lax.broadcasted_iota(I32, (tm, tm), 1)
    upper = (rr < cc).astype(BF16)
    before = jnp.dot(sel.astype(BF16), upper, preferred_element_type=F32).astype(I32) + base
    for k in range(TOP_K):
        rank_ref[k:k + 1, :] = jnp.sum(jnp.where(hits[k], before, 0), axis=0, keepdims=True)
    new = base + jnp.sum(sel, axis=1, keepdims=True).astype(I32)
    carry[...] = new
    cnt_ref[...] = new


def _post(a, p, x, g1, ng, sh, sc, woa, wop, wr1, wr2, rb):
    bsz, seq, d = x.shape
    da = a.shape[2]
    dp = p.shape[2]
    n_exp = rb.shape[0]
    tm = min(512, seq)
    nt = seq // tm
    tokens = bsz * seq
    kern = functools.partial(_post_kernel, tm=tm, n_exp=n_exp)
    row = lambda b, i: (b, i, 0)
    per_b = lambda b, i: (b, 0, 0)
    const2 = lambda b, i: (0, 0)
    tok = lambda b, i: (0, b * nt + i)
    return pl.pallas_call(
        kern,
        out_shape=(jax.ShapeDtypeStruct((bsz, seq, d), F32),
                   jax.ShapeDtypeStruct((bsz, seq, d), F32),
                   jax.ShapeDtypeStruct((TOP_K, tokens), I32),
                   jax.ShapeDtypeStruct((TOP_K, tokens), F32),
                   jax.ShapeDtypeStruct((TOP_K, tokens), I32),
                   jax.ShapeDtypeStruct((n_exp, 1), I32)),
        grid=(bsz, nt),
        in_specs=[pl.BlockSpec((1, tm, da), row),
                  pl.BlockSpec((1, tm, dp), row),
                  pl.BlockSpec((1, tm, d), row),
                  pl.BlockSpec((1, 1, d), per_b),
                  pl.BlockSpec((1, d), const2),
                  pl.BlockSpec((1, 1, d), per_b),
                  pl.BlockSpec((1, 1, d), per_b),
                  _resident((da, d), const2),
                  _resident((dp, d), const2),
                  _resident((d, 2 * LANES), const2),
                  _resident((d, LANES), const2),
                  pl.BlockSpec((n_exp, 1), const2)],
        out_specs=(pl.BlockSpec((1, tm, d), row),
                   pl.BlockSpec((1, tm, d), row),
                   pl.BlockSpec((TOP_K, tm), tok),
                   pl.BlockSpec((TOP_K, tm), tok),
                   pl.BlockSpec((TOP_K, tm), tok),
                   pl.BlockSpec((n_exp, 1), const2)),
        scratch_shapes=[pltpu.VMEM((n_exp, 1), I32)],
        compiler_params=_cparams(("arbitrary", "arbitrary")),
        name="post",
    )(a, p, x, g1, ng, sh, sc, woa, wop, wr1, wr2, rb)


def _row_copy(src, s, dst, t, sem):
    return pltpu.make_async_copy(src.at[pl.ds(s, 1)], dst.at[pl.ds(t, 1)], sem)


def _dispatch_kernel(pstart_ref, cnt_ref, idx_ref, rank_ref, h_ref, xs_ref, pos_ref,
                     zrow, sem, zsem, *, td, n_exp):
    step = pl.program_id(0)

    @pl.when(step == 0)
    def _():
        zrow[...] = jnp.zeros_like(zrow)

        def per_expert(e, total):
            lo = pstart_ref[e] + cnt_ref[e]
            hi = pstart_ref[e + 1]

            def fill(r, c):
                _row_copy(zrow, 0, xs_ref, r, zsem).start()
                return c

            lax.fori_loop(lo, hi, fill, 0)
            return total + (hi - lo)

        npad = lax.fori_loop(0, n_exp, per_expert, 0)

        def drain(r, c):
            _row_copy(zrow, 0, xs_ref, 0, zsem).wait()
            return c

        lax.fori_loop(0, npad, drain, 0)

    t0 = step * td

    def per_token(t, c):
        for k in range(TOP_K):
            pos = pstart_ref[idx_ref[k, t]] + rank_ref[k, t]
            pos_ref[k, t] = pos
            _row_copy(h_ref, t0 + t, xs_ref, pos, sem).start()
        return c

    lax.fori_loop(0, td, per_token, 0)

    def drain_tok(t, c):
        for k in range(TOP_K):
            _row_copy(h_ref, 0, xs_ref, 0, sem).wait()
        return c

    lax.fori_loop(0, td, drain_tok, 0)


def _dispatch(pstart, counts, idx_t, rank_t, h2, n_rows):
    tokens, d = h2.shape
    n_exp = counts.shape[0]
    td = min(512, tokens)
    kern = functools.partial(_dispatch_kernel, td=td, n_exp=n_exp)
    smem_blk = pl.BlockSpec((TOP_K, td), lambda i, *_: (0, i), memory_space=pltpu.SMEM)
    return pl.pallas_call(
        kern,
        out_shape=(jax.ShapeDtypeStruct((n_rows, d), h2.dtype),
                   jax.ShapeDtypeStruct((TOP_K, tokens), I32)),
        grid_spec=pltpu.PrefetchScalarGridSpec(
            num_scalar_prefetch=2,
            grid=(tokens // td,),
            in_specs=[smem_blk, smem_blk, pl.BlockSpec(memory_space=pl.ANY)],
            out_specs=(pl.BlockSpec(memory_space=pl.ANY), smem_blk),
            scratch_shapes=[pltpu.VMEM((8, d), h2.dtype),
                            pltpu.SemaphoreType.DMA, pltpu.SemaphoreType.DMA]),
        compiler_params=_cparams(("arbitrary",)),
        name="dispatch",
    )(pstart, counts, idx_t, rank_t, h2)


def _experts_kernel(bexp_ref, nused_ref, x_ref, w1_ref, w3_ref, w2_ref, y_ref):
    @pl.when(pl.program_id(0) < nused_ref[0])
    def _():
        xb = x_ref[...].astype(BF16)
        h1 = jnp.dot(xb, w1_ref[0], preferred_element_type=F32)
        h3 = jnp.dot(xb, w3_ref[0], preferred_element_type=F32)
        act = (h1 * jax.nn.sigmoid(h1) * h3).astype(BF16)
        y_ref[...] = jnp.dot(act, w2_ref[0], preferred_element_type=F32)


def _experts(block_exp, nused, xs, w1, w3, w2):
    n_rows, d = xs.shape
    de = w1.shape[2]
    nb = n_rows // EXPERT_BLOCK
    blk = lambda b, be, nu: (jnp.minimum(b, nu[0] - 1), 0)
    wsel = lambda b, be, nu: (be[jnp.minimum(b, nu[0] - 1)], 0, 0)
    return pl.pallas_call(
        _experts_kernel,
        out_shape=jax.ShapeDtypeStruct((n_rows, d), F32),
        grid_spec=pltpu.PrefetchScalarGridSpec(
            num_scalar_prefetch=2,
            grid=(nb,),
            in_specs=[pl.BlockSpec((EXPERT_BLOCK, d), blk),
                      pl.BlockSpec((1, d, de), wsel),
                      pl.BlockSpec((1, d, de), wsel),
                      pl.BlockSpec((1, de, d), wsel)],
            out_specs=pl.BlockSpec((EXPERT_BLOCK, d), blk)),
        compiler_params=_cparams(("arbitrary",)),
        name="experts",
    )(block_exp, nused, xs, w1, w3, w2)


def _combine_kernel(pos_ref, y_ref, wts_ref, h_ref, x1_ref, g2_ref, w1_ref, w3_ref, w2_ref,
                    o_ref, buf, sem, *, tc):
    def per_token(t, c):
        for k in range(TOP_K):
            pltpu.make_async_copy(y_ref.at[pl.ds(pos_ref[k, t], 1)],
                                  buf.at[k, pl.ds(t, 1)], sem).start()
        return c

    lax.fori_loop(0, tc, per_token, 0)

    hb = h_ref[0].astype(BF16)
    h1 = jnp.dot(hb, w1_ref[...], preferred_element_type=F32)
    h3 = jnp.dot(hb, w3_ref[...], preferred_element_type=F32)
    act = (h1 * jax.nn.sigmoid(h1) * h3).astype(BF16)
    shared = jnp.dot(act, w2_ref[...], preferred_element_type=F32)

    def drain(t, c):
        for k in range(TOP_K):
            pltpu.make_async_copy(y_ref.at[pl.ds(0, 1)], buf.at[k, pl.ds(0, 1)], sem).wait()
        return c

    lax.fori_loop(0, tc, drain, 0)

    w = wts_ref[...]
    routed = buf[0] * w[:, 0:1]
    for k in range(1, TOP_K):
        routed = routed + buf[k] * w[:, k:k + 1]
    o_ref[0] = x1_ref[0] + g2_ref[0] * (routed + shared)


def _combine(pos_t, y, wts_tok, h2, x1, g2, w1s, w3s, w2s):
    bsz, seq, d = x1.shape
    de = w1s.shape[1]
    tc = min(256, seq)
    nt = seq // tc
    kern = functools.partial(_combine_kernel, tc=tc)
    row = lambda b, i: (b, i, 0)
    const2 = lambda b, i: (0, 0)
    return pl.pallas_call(
        kern,
        out_shape=jax.ShapeDtypeStruct((bsz, seq, d), F32),
        grid=(bsz, nt),
        in_specs=[pl.BlockSpec((TOP_K, tc), lambda b, i: (0, b * nt + i),
                               memory_space=pltpu.SMEM),
                  pl.BlockSpec(memory_space=pl.ANY),
                  pl.BlockSpec((tc, TOP_K), lambda b, i: (b * nt + i, 0)),
                  pl.BlockSpec((1, tc, d), row),
                  pl.BlockSpec((1, tc, d), row),
                  pl.BlockSpec((1, 1, d), lambda b, i: (b, 0, 0)),
                  _resident((d, de), const2),
                  _resident((d, de), const2),
                  _resident((de, d), const2)],
        out_specs=pl.BlockSpec((1, tc, d), row),
        scratch_shapes=[pltpu.VMEM((TOP_K, tc, d), F32), pltpu.SemaphoreType.DMA],
        compiler_params=_cparams(("arbitrary", "arbitrary")),
        name="combine",
    )(pos_t, y, wts_tok, h2, x1, g2, w1s, w3s, w2s)


def _final_kernel(x_ref, g_ref, o_ref):
    x = x_ref[0]
    ms = jnp.mean(x * x, axis=-1, keepdims=True)
    o_ref[0] = (x * lax.rsqrt(ms + EPS)) * g_ref[...]


def _final_norm(x, g):
    bsz, seq, d = x.shape
    tm = min(512, seq)
    row = lambda b, i: (b, i, 0)
    return pl.pallas_call(
        _final_kernel,
        out_shape=jax.ShapeDtypeStruct(x.shape, F32),
        grid=(bsz, seq // tm),
        in_specs=[pl.BlockSpec((1, tm, d), row), pl.BlockSpec((1, d), lambda b, i: (0, 0))],
        out_specs=pl.BlockSpec((1, tm, d), row),
        compiler_params=_cparams(("arbitrary", "arbitrary")),
        name="final_norm",
    )(x, g)


def _pad_lanes(w, n):
    return jnp.pad(w, ((0, 0), (0, n - w.shape[1])))


def kernel(x, c, norm1_g, norm2_g, w_ada, b_ada, w_in, b_f, w_pool, pool_scale, w_o,
           w_router, router_bias, w1, w3, w2, w1_shared, w3_shared, w2_shared, final_g):
    bsz, seq, d = x.shape
    depth = w_ada.shape[0]
    n_heads = b_f.shape[1]
    da = n_heads * HEAD_DIM
    dp = d - da
    n_exp = w_router.shape[2]
    tokens = bsz * seq
    n_assign = tokens * TOP_K
    n_blocks = -(-n_assign // EXPERT_BLOCK) + n_exp
    n_rows = n_blocks * EXPERT_BLOCK

    mod = _ada(c, w_ada, b_ada)
    mod = mod.reshape(depth, bsz, N_MOD, 1, d)

    for l in range(depth):
        sh1, sc1, g1, sh2, sc2, g2 = (mod[l, :, j] for j in range(N_MOD))
        wl = w_in[l]
        wq = wl[:, :da].astype(BF16)
        wk = wl[:, da:2 * da].astype(BF16)
        wv = wl[:, 2 * da:3 * da].astype(BF16)
        wf = _pad_lanes(wl[:, 3 * da:3 * da + n_heads], LANES).astype(BF16)
        wu = wl[:, 3 * da + n_heads:].astype(BF16)
        bf = _pad_lanes(b_f[l][None, :], LANES)
        q, k, v, dcol, drow, p = _proj(
            x, norm1_g[l][None, :], sh1, sc1, wq, wk, wv, wf, bf, wu,
            w_pool[l].astype(BF16), pool_scale[l][None, :], n_heads=n_heads)
        a = _attn(q, k, v, dcol, drow, n_heads=n_heads)

        wr = _pad_lanes(w_router[l], LANES)
        wr_hi = wr.astype(BF16)
        wr_lo = (wr - wr_hi.astype(F32)).astype(BF16)
        wo = w_o[l].astype(BF16)
        x1, h2, idx_t, wts_t, rank_t, counts = _post(
            a, p, x, g1, norm2_g[l][None, :], sh2, sc2, wo[:da], wo[da:],
            jnp.concatenate([wr_hi, wr_lo], axis=1), wr_hi, router_bias[l][:, None])

        counts = counts[:, 0]
        padded = ((counts + EXPERT_BLOCK - 1) // EXPERT_BLOCK) * EXPERT_BLOCK
        cum = jnp.cumsum(padded)
        pstart = jnp.concatenate([jnp.zeros((1,), I32), cum]).astype(I32)
        nused = (cum[-1:] // EXPERT_BLOCK).astype(I32)
        block_starts = jnp.arange(n_blocks, dtype=I32) * EXPERT_BLOCK
        block_exp = jnp.minimum(jnp.searchsorted(cum, block_starts, side='right'),
                                n_exp - 1).astype(I32)

        h2f = h2.reshape(tokens, d)
        xs, pos_t = _dispatch(pstart, counts, idx_t, rank_t, h2f, n_rows)
        y = _experts(block_exp, nused, xs, w1[l].astype(BF16), w3[l].astype(BF16),
                     w2[l].astype(BF16))
        x = _combine(pos_t, y, wts_t.T, h2, x1, g2, w1_shared[l].astype(BF16),
                     w3_shared[l].astype(BF16), w2_shared[l].astype(BF16))
    return _final_norm(x, final_g[None, :])
```

```python
import functools

import jax
import jax.numpy as jnp
from jax import lax
from jax.experimental import pallas as pl
from jax.experimental.pallas import tpu as pltpu

F32 = jnp.float32
BF16 = jnp.bfloat16
I32 = jnp.int32

EPS = 1e-6
HEAD_DIM = 128
LANES = 128
POOL_WINDOWS = (2, 4, 8, 16)
POOL_HALO = 16
TOP_K = 8
N_EXPERT_GROUPS = 8
TOPK_GROUPS = 4
ROUTED_SCALE = 2.5
EXPERT_BLOCK = 512
N_MOD = 6
VMEM_LIMIT = 56 * 1024 * 1024

_NT = (((1,), (1,)), ((), ()))


def _cparams(sem):
    return pltpu.CompilerParams(dimension_semantics=sem, vmem_limit_bytes=VMEM_LIMIT)


def _resident(shape, index_map):
    return pl.BlockSpec(shape, index_map, pipeline_mode=pl.Buffered(1))


def _ada_kernel(c_ref, w_ref, b_ref, o_ref):
    c = c_ref[...]
    ca = c * jax.nn.sigmoid(c)
    o_ref[0] = jnp.dot(ca, w_ref[0], precision=lax.Precision.HIGHEST,
                       preferred_element_type=F32) + b_ref[0]


def _ada(c, w_ada, b_ada):
    depth, d, n = w_ada.shape
    bsz = c.shape[0]
    rows = 8
    cp = jnp.zeros((rows, d), F32).at[:bsz].set(c)
    tn = 512
    out = pl.pallas_call(
        _ada_kernel,
        out_shape=jax.ShapeDtypeStruct((depth, rows, n), F32),
        grid=(depth, n // tn),
        in_specs=[pl.BlockSpec((rows, d), lambda l, j: (0, 0)),
                  pl.BlockSpec((1, d, tn), lambda l, j: (l, 0, j)),
                  pl.BlockSpec((1, 1, tn), lambda l, j: (l, 0, j))],
        out_specs=pl.BlockSpec((1, rows, tn), lambda l, j: (l, 0, j)),
        compiler_params=_cparams(("arbitrary", "arbitrary")),
        name="ada",
    )(cp, w_ada, b_ada.reshape(depth, 1, n))
    return out[:, :bsz]


def _proj_kernel(x_ref, g_ref, sh_ref, sc_ref, wq_ref, wk_ref, wv_ref, wf_ref, bf_ref,
                 wu_ref, wp_ref, ps_ref,
                 q_ref, k_ref, v_ref, dcol_ref, drow_ref, p_ref,
                 ubuf, dcarry, *, tm, n_heads, gc):
    i = pl.program_id(1)
    x = x_ref[0]
    ms = jnp.mean(x * x, axis=-1, keepdims=True)
    y = (x * lax.rsqrt(ms + EPS)) * g_ref[...]
    h = y * (1.0 + sc_ref[0]) + sh_ref[0]
    hb = h.astype(BF16)

    scale = HEAD_DIM ** -0.5
    q_ref[0] = (jnp.dot(hb, wq_ref[...], preferred_element_type=F32) * scale).astype(BF16)
    k_ref[0] = jnp.dot(hb, wk_ref[...], preferred_element_type=F32).astype(BF16)
    v_ref[0] = jnp.dot(hb, wv_ref[...], preferred_element_type=F32).astype(BF16)

    z = jnp.dot(hb, wf_ref[...], preferred_element_type=F32) + bf_ref[...]
    lf = jnp.minimum(z, 0.0) - jnp.log1p(jnp.exp(-jnp.abs(z)))
    r = lax.broadcasted_iota(I32, (tm, tm), 0)
    cidx = lax.broadcasted_iota(I32, (tm, tm), 1)
    tri = (cidx <= r).astype(F32)

    @pl.when(i == 0)
    def _():
        dcarry[...] = jnp.zeros_like(dcarry)
        ubuf[0:POOL_HALO, :] = jnp.zeros((POOL_HALO, ubuf.shape[1]), F32)

    @pl.when(i > 0)
    def _():
        ubuf[0:POOL_HALO, :] = ubuf[tm:tm + POOL_HALO, :]

    dc = jnp.dot(tri, lf, precision=lax.Precision.HIGHEST,
                 preferred_element_type=F32) + dcarry[...]
    dcarry[...] = dc[tm - 1:tm, :]
    dcol_ref[0] = dc
    drow_ref[0] = dc.T[0:drow_ref.shape[1], :]

    u = jnp.dot(hb, wu_ref[...], preferred_element_type=F32)
    ubuf[POOL_HALO:POOL_HALO + tm, :] = u
    pos = i * tm + lax.broadcasted_iota(I32, (tm, 1), 0)
    for g, w in enumerate(POOL_WINDOWS):
        s = ubuf[:, g * gc:(g + 1) * gc]
        step = 1
        while step < w:
            s = s + pltpu.roll(s, step, 0)
            step *= 2
        cnt = jnp.minimum(pos + 1, w).astype(F32)
        pooled = s[POOL_HALO:POOL_HALO + tm, :] / cnt - u[:, g * gc:(g + 1) * gc]
        yg = jnp.dot(pooled.astype(BF16), wp_ref[g], preferred_element_type=F32)
        p_ref[0, :, g * gc:(g + 1) * gc] = (yg * ps_ref[:, g * gc:(g + 1) * gc]).astype(BF16)


def _proj(x, g, sh, sc, wq, wk, wv, wf, bf, wu, wp, ps, *, n_heads):
    bsz, seq, d = x.shape
    da = wq.shape[1]
    dp = wu.shape[1]
    gc = dp // len(POOL_WINDOWS)
    tm = min(512, seq)
    nt = seq // tm
    kern = functools.partial(_proj_kernel, tm=tm, n_heads=n_heads, gc=gc)
    row = lambda b, i: (b, i, 0)
    per_b = lambda b, i: (b, 0, 0)
    const2 = lambda b, i: (0, 0)
    const3 = lambda b, i: (0, 0, 0)
    return pl.pallas_call(
        kern,
        out_shape=(jax.ShapeDtypeStruct((bsz, seq, da), BF16),
                   jax.ShapeDtypeStruct((bsz, seq, da), BF16),
                   jax.ShapeDtypeStruct((bsz, seq, da), BF16),
                   jax.ShapeDtypeStruct((bsz, seq, LANES), F32),
                   jax.ShapeDtypeStruct((bsz, 8, seq), F32),
                   jax.ShapeDtypeStruct((bsz, seq, dp), BF16)),
        grid=(bsz, nt),
        in_specs=[pl.BlockSpec((1, tm, d), row),
                  pl.BlockSpec((1, d), const2),
                  pl.BlockSpec((1, 1, d), per_b),
                  pl.BlockSpec((1, 1, d), per_b),
                  _resident((d, da), const2),
                  _resident((d, da), const2),
                  _resident((d, da), const2),
                  _resident((d, LANES), const2),
                  pl.BlockSpec((1, LANES), const2),
                  _resident((d, dp), const2),
                  _resident(wp.shape, const3),
                  pl.BlockSpec((1, dp), const2)],
        out_specs=(pl.BlockSpec((1, tm, da), row),
                   pl.BlockSpec((1, tm, da), row),
                   pl.BlockSpec((1, tm, da), row),
                   pl.BlockSpec((1, tm, LANES), row),
                   pl.BlockSpec((1, 8, tm), lambda b, i: (b, 0, i)),
                   pl.BlockSpec((1, tm, dp), row)),
        scratch_shapes=[pltpu.VMEM((tm + POOL_HALO, dp), F32),
                        pltpu.VMEM((1, LANES), F32)],
        compiler_params=_cparams(("arbitrary", "arbitrary")),
        name="proj",
    )(x, g, sh, sc, wq, wk, wv, wf, bf, wu, wp, ps)


def _attn_kernel(q_ref, k_ref, v_ref, dcol_ref, drow_ref, o_ref, *, tq, hp):
    hg = pl.program_id(1)
    i = pl.program_id(2)
    lane = lax.broadcasted_iota(I32, (tq, LANES), 1)
    dcol = dcol_ref[0]
    heads = [slice(u * HEAD_DIM, (u + 1) * HEAD_DIM) for u in range(hp)]
    qs = [q_ref[0, :, hs] for hs in heads]
    dqs = [jnp.sum(jnp.where(lane == hg * hp + u, dcol, 0.0), axis=-1, keepdims=True)
           for u in range(hp)]

    def block(j, carry, masked):
        start = pl.multiple_of(j * tq, tq)
        out = []
        for u in range(hp):
            m, l, acc = carry[u]
            ks = k_ref[0, pl.ds(start, tq), heads[u]]
            vs = v_ref[0, pl.ds(start, tq), heads[u]]
            dk = drow_ref[0, pl.ds(hg * hp + u, 1), pl.ds(start, tq)]
            z = lax.dot_general(qs[u], ks, _NT, preferred_element_type=F32) - dk
            if masked:
                rr = lax.broadcasted_iota(I32, (tq, tq), 0)
                cc = lax.broadcasted_iota(I32, (tq, tq), 1)
                z = jnp.where(cc <= rr, z, -jnp.inf)
            m_new = jnp.maximum(m, jnp.max(z, axis=-1, keepdims=True) + dqs[u])
            p = jnp.exp(z - (m_new - dqs[u]))
            alpha = jnp.exp(m - m_new)
            l = alpha * l + jnp.sum(p, axis=-1, keepdims=True)
            acc = alpha * acc + jnp.dot(p.astype(BF16), vs, preferred_element_type=F32)
            out.append((m_new, l, acc))
        return tuple(out)

    init = tuple((jnp.full((tq, 1), -jnp.inf, F32), jnp.zeros((tq, 1), F32),
                  jnp.zeros((tq, HEAD_DIM), F32)) for _ in range(hp))
    carry = lax.fori_loop(0, i, functools.partial(block, masked=False), init)
    carry = block(i, carry, True)
    for u in range(hp):
        _, l, acc = carry[u]
        o_ref[0, :, heads[u]] = (acc / l).astype(o_ref.dtype)


def _attn(q, k, v, dcol, drow, *, n_heads):
    bsz, seq, da = q.shape
    tq = min(512, seq)
    nq = seq // tq
    hp = 2 if n_heads % 2 == 0 else 1
    kern = functools.partial(_attn_kernel, tq=tq, hp=hp)
    return pl.pallas_call(
        kern,
        out_shape=jax.ShapeDtypeStruct((bsz, seq, da), BF16),
        grid=(bsz, n_heads // hp, nq),
        in_specs=[pl.BlockSpec((1, tq, hp * HEAD_DIM), lambda b, h, i: (b, i, h)),
                  pl.BlockSpec((1, seq, hp * HEAD_DIM), lambda b, h, i: (b, 0, h)),
                  pl.BlockSpec((1, seq, hp * HEAD_DIM), lambda b, h, i: (b, 0, h)),
                  pl.BlockSpec((1, tq, LANES), lambda b, h, i: (b, i, 0)),
                  pl.BlockSpec((1, 8, seq), lambda b, h, i: (b, 0, 0))],
        out_specs=pl.BlockSpec((1, tq, hp * HEAD_DIM), lambda b, h, i: (b, i, h)),
        compiler_params=_cparams(("arbitrary", "arbitrary", "arbitrary")),
        name="attn",
    )(q, k, v, dcol, drow)


def _post_kernel(a_ref, p_ref, x_ref, g1_ref, ng_ref, sh_ref, sc_ref, woa_ref, wop_ref,
                 wr1_ref, wr2_ref, rb_ref,
                 x1_ref, h2_ref, idx_ref, wts_ref, rank_ref, cnt_ref,
                 carry, *, tm, n_exp):
    first = jnp.logical_and(pl.program_id(0) == 0, pl.program_id(1) == 0)

    @pl.when(first)
    def _():
        carry[...] = jnp.zeros_like(carry)

    mix = jnp.dot(a_ref[0], woa_ref[...], preferred_element_type=F32)
    mix = mix + jnp.dot(p_ref[0], wop_ref[...], preferred_element_type=F32)
    x1 = x_ref[0] + g1_ref[0] * mix
    x1_ref[0] = x1
    ms = jnp.mean(x1 * x1, axis=-1, keepdims=True)
    y = (x1 * lax.rsqrt(ms + EPS)) * ng_ref[...]
    h2 = y * (1.0 + sc_ref[0]) + sh_ref[0]
    h2_ref[0] = h2

    h_hi = h2.astype(BF16)
    h_lo = (h2 - h_hi.astype(F32)).astype(BF16)
    a1 = jnp.dot(h_hi, wr1_ref[...], preferred_element_type=F32)
    a2 = jnp.dot(h_lo, wr2_ref[...], preferred_element_type=F32)
    logits = a1[:, :LANES] + a1[:, LANES:] + a2
    lt = logits.T[0:n_exp, :]
    sc = jax.nn.sigmoid(lt)
    biased = sc + rb_ref[...]

    epg = n_exp // N_EXPERT_GROUPS
    sub = lax.broadcasted_iota(I32, (epg, tm), 0)
    blocks, gscore = [], []
    for g in range(N_EXPERT_GROUPS):
        blk = biased[g * epg:(g + 1) * epg, :]
        m1 = jnp.max(blk, axis=0, keepdims=True)
        f1 = jnp.min(jnp.where(blk == m1, sub, epg), axis=0, keepdims=True)
        m2 = jnp.max(jnp.where(sub == f1, -jnp.inf, blk), axis=0, keepdims=True)
        blocks.append(blk)
        gscore.append(m1 + m2)
    masked = []
    for g in range(N_EXPERT_GROUPS):
        beats = jnp.zeros((1, tm), I32)
        for g2 in range(N_EXPERT_GROUPS):
            if g2 == g:
                continue
            win = (gscore[g2] > gscore[g]) if g2 > g else (gscore[g2] >= gscore[g])
            beats = beats + win.astype(I32)
        masked.append(jnp.where(beats < TOPK_GROUPS, blocks[g], -jnp.inf))
    v = jnp.concatenate(masked, axis=0)

    eio = lax.broadcasted_iota(I32, (n_exp, tm), 0)
    base = carry[...]
    hits, svals = [], []
    sel = jnp.zeros((n_exp, tm), F32)
    for k in range(TOP_K):
        m = jnp.max(v, axis=0, keepdims=True)
        idx = jnp.min(jnp.where(v == m, eio, n_exp), axis=0, keepdims=True)
        hit = eio == idx
        svals.append(jnp.sum(jnp.where(hit, sc, 0.0), axis=0, keepdims=True))
        v = jnp.where(hit, -jnp.inf, v)
        sel = sel + hit.astype(F32)
        hits.append(hit)
        idx_ref[k:k + 1, :] = idx
    total = svals[0]
    for k in range(1, TOP_K):
        total = total + svals[k]
    for k in range(TOP_K):
        wts_ref[k:k + 1, :] = svals[k] / total * ROUTED_SCALE

    rr = lax.broadcasted_iota(I32, (tm, tm), 0)
    cc = lax.broadcasted_iota(I32, (tm, tm), 1)
    upper = (rr < cc).astype(BF16)
    before = jnp.dot(sel.astype(BF16), upper, preferred_element_type=F32).astype(I32) + base
    for k in range(TOP_K):
        rank_ref[k:k + 1, :] = jnp.sum(jnp.where(hits[k], before, 0), axis=0, keepdims=True)
    new = base + jnp.sum(sel, axis=1, keepdims=True).astype(I32)
    carry[...] = new
    cnt_ref[...] = new


def _post(a, p, x, g1, ng, sh, sc, woa, wop, wr1, wr2, rb):
    bsz, seq, d = x.shape
    da = a.shape[2]
    dp = p.shape[2]
    n_exp = rb.shape[0]
    tm = min(512, seq)
    nt = seq // tm
    tokens = bsz * seq
    kern = functools.partial(_post_kernel, tm=tm, n_exp=n_exp)
    row = lambda b, i: (b, i, 0)
    per_b = lambda b, i: (b, 0, 0)
    const2 = lambda b, i: (0, 0)
    tok = lambda b, i: (0, b * nt + i)
    return pl.pallas_call(
        kern,
        out_shape=(jax.ShapeDtypeStruct((bsz, seq, d), F32),
                   jax.ShapeDtypeStruct((bsz, seq, d), F32),
                   jax.ShapeDtypeStruct((TOP_K, tokens), I32),
                   jax.ShapeDtypeStruct((TOP_K, tokens), F32),
                   jax.ShapeDtypeStruct((TOP_K, tokens), I32),
                   jax.ShapeDtypeStruct((n_exp, 1), I32)),
        grid=(bsz, nt),
        in_specs=[pl.BlockSpec((1, tm, da), row),
                  pl.BlockSpec((1, tm, dp), row),
                  pl.BlockSpec((1, tm, d), row),
                  pl.BlockSpec((1, 1, d), per_b),
                  pl.BlockSpec((1, d), const2),
                  pl.BlockSpec((1, 1, d), per_b),
                  pl.BlockSpec((1, 1, d), per_b),
                  _resident((da, d), const2),
                  _resident((dp, d), const2),
                  _resident((d, 2 * LANES), const2),
                  _resident((d, LANES), const2),
                  pl.BlockSpec((n_exp, 1), const2)],
        out_specs=(pl.BlockSpec((1, tm, d), row),
                   pl.BlockSpec((1, tm, d), row),
                   pl.BlockSpec((TOP_K, tm), tok),
                   pl.BlockSpec((TOP_K, tm), tok),
                   pl.BlockSpec((TOP_K, tm), tok),
                   pl.BlockSpec((n_exp, 1), const2)),
        scratch_shapes=[pltpu.VMEM((n_exp, 1), I32)],
        compiler_params=_cparams(("arbitrary", "arbitrary")),
        name="post",
    )(a, p, x, g1, ng, sh, sc, woa, wop, wr1, wr2, rb)


def _row_copy(src, s, dst, t, sem):
    return pltpu.make_async_copy(src.at[pl.ds(s, 1)], dst.at[pl.ds(t, 1)], sem)


def _dispatch_kernel(pstart_ref, cnt_ref, idx_ref, rank_ref, h_ref, xs_ref, pos_ref,
                     zrow, sem, zsem, *, td, n_exp):
    step = pl.program_id(0)

    @pl.when(step == 0)
    def _():
        zrow[...] = jnp.zeros_like(zrow)

        def per_expert(e, total):
            lo = pstart_ref[e] + cnt_ref[e]
            hi = pstart_ref[e + 1]

            def fill(r, c):
                _row_copy(zrow, 0, xs_ref, r, zsem).start()
                return c

            lax.fori_loop(lo, hi, fill, 0)
            return total + (hi - lo)

        npad = lax.fori_loop(0, n_exp, per_expert, 0)

        def drain(r, c):
            _row_copy(zrow, 0, xs_ref, 0, zsem).wait()
            return c

        lax.fori_loop(0, npad, drain, 0)

    def per_token(t, c):
        for k in range(TOP_K):
            pos = pstart_ref[idx_ref[k, t]] + rank_ref[k, t]
            pos_ref[k, t] = pos
            _row_copy(h_ref, t, xs_ref, pos, sem).start(priority=k % 2)
        return c

    lax.fori_loop(0, td, per_token, 0, unroll=2)

    for k in range(TOP_K):
        pltpu.make_async_copy(h_ref, xs_ref.at[pl.ds(0, td)], sem).wait()


def _dispatch(pstart, counts, idx_t, rank_t, h2, n_rows):
    tokens, d = h2.shape
    n_exp = counts.shape[0]
    td = min(512, tokens)
    kern = functools.partial(_dispatch_kernel, td=td, n_exp=n_exp)
    smem_blk = pl.BlockSpec((TOP_K, td), lambda i, *_: (0, i), memory_space=pltpu.SMEM)
    return pl.pallas_call(
        kern,
        out_shape=(jax.ShapeDtypeStruct((n_rows, d), h2.dtype),
                   jax.ShapeDtypeStruct((TOP_K, tokens), I32)),
        grid_spec=pltpu.PrefetchScalarGridSpec(
            num_scalar_prefetch=2,
            grid=(tokens // td,),
            in_specs=[smem_blk, smem_blk, pl.BlockSpec((td, d), lambda i, *_: (i, 0))],
            out_specs=(pl.BlockSpec(memory_space=pl.ANY), smem_blk),
            scratch_shapes=[pltpu.VMEM((8, d), h2.dtype),
                            pltpu.SemaphoreType.DMA, pltpu.SemaphoreType.DMA]),
        compiler_params=_cparams(("arbitrary",)),
        name="dispatch",
    )(pstart, counts, idx_t, rank_t, h2)


def _experts_kernel(bexp_ref, nused_ref, x_ref, w1_ref, w3_ref, w2_ref, y_ref):
    @pl.when(pl.program_id(0) < nused_ref[0])
    def _():
        xb = x_ref[...].astype(BF16)
        h1 = jnp.dot(xb, w1_ref[0], preferred_element_type=F32)
        h3 = jnp.dot(xb, w3_ref[0], preferred_element_type=F32)
        act = (h1 * jax.nn.sigmoid(h1) * h3).astype(BF16)
        y_ref[...] = jnp.dot(act, w2_ref[0], preferred_element_type=F32)


def _experts(block_exp, nused, xs, w1, w3, w2):
    n_rows, d = xs.shape
    de = w1.shape[2]
    nb = n_rows // EXPERT_BLOCK
    blk = lambda b, be, nu: (jnp.minimum(b, nu[0] - 1), 0)
    wsel = lambda b, be, nu: (be[jnp.minimum(b, nu[0] - 1)], 0, 0)
    return pl.pallas_call(
        _experts_kernel,
        out_shape=jax.ShapeDtypeStruct((n_rows, d), F32),
        grid_spec=pltpu.PrefetchScalarGridSpec(
            num_scalar_prefetch=2,
            grid=(nb,),
            in_specs=[pl.BlockSpec((EXPERT_BLOCK, d), blk),
                      pl.BlockSpec((1, d, de), wsel),
                      pl.BlockSpec((1, d, de), wsel),
                      pl.BlockSpec((1, de, d), wsel)],
            out_specs=pl.BlockSpec((EXPERT_BLOCK, d), blk)),
        compiler_params=_cparams(("arbitrary",)),
        name="experts",
    )(block_exp, nused, xs, w1, w3, w2)


def _combine_kernel(pos_ref, y_ref, wts_ref, h_ref, x1_ref, g2_ref, w1_ref, w3_ref, w2_ref,
                    o_ref, buf, sem, *, tc):
    def per_token(t, c):
        for k in range(TOP_K):
            pltpu.make_async_copy(y_ref.at[pl.ds(pos_ref[k, t], 1)],
                                  buf.at[k, pl.ds(t, 1)], sem).start(priority=k % 2)
        return c

    lax.fori_loop(0, tc, per_token, 0, unroll=2)

    hb = h_ref[0].astype(BF16)
    h1 = jnp.dot(hb, w1_ref[...], preferred_element_type=F32)
    h3 = jnp.dot(hb, w3_ref[...], preferred_element_type=F32)
    act = (h1 * jax.nn.sigmoid(h1) * h3).astype(BF16)
    shared = jnp.dot(act, w2_ref[...], preferred_element_type=F32)

    for k in range(TOP_K):
        pltpu.make_async_copy(y_ref.at[pl.ds(0, tc)], buf.at[k], sem).wait()

    w = wts_ref[...]
    routed = buf[0] * w[:, 0:1]
    for k in range(1, TOP_K):
        routed = routed + buf[k] * w[:, k:k + 1]
    o_ref[0] = x1_ref[0] + g2_ref[0] * (routed + shared)


def _combine(pos_t, y, wts_tok, h2, x1, g2, w1s, w3s, w2s):
    bsz, seq, d = x1.shape
    de = w1s.shape[1]
    tc = min(256, seq)
    nt = seq // tc
    kern = functools.partial(_combine_kernel, tc=tc)
    row = lambda b, i: (b, i, 0)
    const2 = lambda b, i: (0, 0)
    return pl.pallas_call(
        kern,
        out_shape=jax.ShapeDtypeStruct((bsz, seq, d), F32),
        grid=(bsz, nt),
        in_specs=[pl.BlockSpec((TOP_K, tc), lambda b, i: (0, b * nt + i),
                               memory_space=pltpu.SMEM),
                  pl.BlockSpec(memory_space=pl.ANY),
                  pl.BlockSpec((tc, TOP_K), lambda b, i: (b * nt + i, 0)),
                  pl.BlockSpec((1, tc, d), row),
                  pl.BlockSpec((1, tc, d), row),
                  pl.BlockSpec((1, 1, d), lambda b, i: (b, 0, 0)),
                  _resident((d, de), const2),
                  _resident((d, de), const2),
                  _resident((de, d), const2)],
        out_specs=pl.BlockSpec((1, tc, d), row),
        scratch_shapes=[pltpu.VMEM((TOP_K, tc, d), F32), pltpu.SemaphoreType.DMA],
        compiler_params=_cparams(("arbitrary", "arbitrary")),
        name="combine",
    )(pos_t, y, wts_tok, h2, x1, g2, w1s, w3s, w2s)


def _final_kernel(x_ref, g_ref, o_ref):
    x = x_ref[0]
    ms = jnp.mean(x * x, axis=-1, keepdims=True)
    o_ref[0] = (x * lax.rsqrt(ms + EPS)) * g_ref[...]


def _final_norm(x, g):
    bsz, seq, d = x.shape
    tm = min(512, seq)
    row = lambda b, i: (b, i, 0)
    return pl.pallas_call(
        _final_kernel,
        out_shape=jax.ShapeDtypeStruct(x.shape, F32),
        grid=(bsz, seq // tm),
        in_specs=[pl.BlockSpec((1, tm, d), row), pl.BlockSpec((1, d), lambda b, i: (0, 0))],
        out_specs=pl.BlockSpec((1, tm, d), row),
        compiler_params=_cparams(("arbitrary", "arbitrary")),
        name="final_norm",
    )(x, g)


def _pad_lanes(w, n):
    return jnp.pad(w, ((0, 0), (0, n - w.shape[1])))


def kernel(x, c, norm1_g, norm2_g, w_ada, b_ada, w_in, b_f, w_pool, pool_scale, w_o,
           w_router, router_bias, w1, w3, w2, w1_shared, w3_shared, w2_shared, final_g):
    bsz, seq, d = x.shape
    depth = w_ada.shape[0]
    n_heads = b_f.shape[1]
    da = n_heads * HEAD_DIM
    dp = d - da
    n_exp = w_router.shape[2]
    tokens = bsz * seq
    n_assign = tokens * TOP_K
    n_blocks = -(-n_assign // EXPERT_BLOCK) + n_exp
    n_rows = n_blocks * EXPERT_BLOCK

    mod = _ada(c, w_ada, b_ada)
    mod = mod.reshape(depth, bsz, N_MOD, 1, d)

    for l in range(depth):
        sh1, sc1, g1, sh2, sc2, g2 = (mod[l, :, j] for j in range(N_MOD))
        wl = w_in[l]
        wq = wl[:, :da].astype(BF16)
        wk = wl[:, da:2 * da].astype(BF16)
        wv = wl[:, 2 * da:3 * da].astype(BF16)
        wf = _pad_lanes(wl[:, 3 * da:3 * da + n_heads], LANES).astype(BF16)
        wu = wl[:, 3 * da + n_heads:].astype(BF16)
        bf = _pad_lanes(b_f[l][None, :], LANES)
        q, k, v, dcol, drow, p = _proj(
            x, norm1_g[l][None, :], sh1, sc1, wq, wk, wv, wf, bf, wu,
            w_pool[l].astype(BF16), pool_scale[l][None, :], n_heads=n_heads)
        a = _attn(q, k, v, dcol, drow, n_heads=n_heads)

        wr = _pad_lanes(w_router[l], LANES)
        wr_hi = wr.astype(BF16)
        wr_lo = (wr - wr_hi.astype(F32)).astype(BF16)
        wo = w_o[l].astype(BF16)
        x1, h2, idx_t, wts_t, rank_t, counts = _post(
            a, p, x, g1, norm2_g[l][None, :], sh2, sc2, wo[:da], wo[da:],
            jnp.concatenate([wr_hi, wr_lo], axis=1), wr_hi, router_bias[l][:, None])

        counts = counts[:, 0]
        padded = ((counts + EXPERT_BLOCK - 1) // EXPERT_BLOCK) * EXPERT_BLOCK
        cum = jnp.cumsum(padded)
        pstart = jnp.concatenate([jnp.zeros((1,), I32), cum]).astype(I32)
        nused = (cum[-1:] // EXPERT_BLOCK).astype(I32)
        block_starts = jnp.arange(n_blocks, dtype=I32) * EXPERT_BLOCK
        block_exp = jnp.minimum(
            jnp.sum((cum[None, :] <= block_starts[:, None]).astype(I32), axis=1), n_exp - 1)

        h2f = h2.reshape(tokens, d)
        xs, pos_t = _dispatch(pstart, counts, idx_t, rank_t, h2f, n_rows)
        y = _experts(block_exp, nused, xs, w1[l].astype(BF16), w3[l].astype(BF16),
                     w2[l].astype(BF16))
        x = _combine(pos_t, y, wts_t.T, h2, x1, g2, w1_shared[l].astype(BF16),
                     w3_shared[l].astype(BF16), w2_shared[l].astype(BF16))
    return _final_norm(x, final_g[None, :])
```

```python
import functools

import jax
import jax.numpy as jnp
from jax import lax
from jax.experimental import pallas as pl
from jax.experimental.pallas import tpu as pltpu

F32 = jnp.float32
BF16 = jnp.bfloat16
I32 = jnp.int32
U32 = jnp.uint32

EPS = 1e-6
HEAD_DIM = 128
LANES = 128
POOL_WINDOWS = (2, 4, 8, 16)
POOL_HALO = 16
TOP_K = 8
N_EXPERT_GROUPS = 8
TOPK_GROUPS = 4
ROUTED_SCALE = 2.5
EXPERT_BLOCK = 512
N_MOD = 6
VMEM_LIMIT = 56 * 1024 * 1024

_NT = (((1,), (1,)), ((), ()))


def _cparams(sem):
    return pltpu.CompilerParams(dimension_semantics=sem, vmem_limit_bytes=VMEM_LIMIT)


def _pack_halves(x):
    n = x.shape[1] // 2
    lo = lax.bitcast_convert_type(x[:, :n].astype(BF16).astype(F32), U32)
    hi = lax.bitcast_convert_type(x[:, n:].astype(BF16).astype(F32), U32)
    return hi | (lo >> 16)


def _unpack_halves(p):
    lo = lax.bitcast_convert_type(p << 16, F32)
    hi = lax.bitcast_convert_type(p & jnp.uint32(0xFFFF0000), F32)
    return lo, hi


def _resident(shape, index_map):
    return pl.BlockSpec(shape, index_map, pipeline_mode=pl.Buffered(1))


def _ada_kernel(c_ref, w_ref, b_ref, o_ref):
    c = c_ref[...]
    ca = c * jax.nn.sigmoid(c)
    o_ref[0] = jnp.dot(ca, w_ref[0], precision=lax.Precision.HIGHEST,
                       preferred_element_type=F32) + b_ref[0]


def _ada(c, w_ada, b_ada):
    depth, d, n = w_ada.shape
    bsz = c.shape[0]
    rows = 8
    cp = jnp.zeros((rows, d), F32).at[:bsz].set(c)
    tn = 512
    out = pl.pallas_call(
        _ada_kernel,
        out_shape=jax.ShapeDtypeStruct((depth, rows, n), F32),
        grid=(depth, n // tn),
        in_specs=[pl.BlockSpec((rows, d), lambda l, j: (0, 0)),
                  pl.BlockSpec((1, d, tn), lambda l, j: (l, 0, j)),
                  pl.BlockSpec((1, 1, tn), lambda l, j: (l, 0, j))],
        out_specs=pl.BlockSpec((1, rows, tn), lambda l, j: (l, 0, j)),
        compiler_params=_cparams(("arbitrary", "arbitrary")),
        name="ada",
    )(cp, w_ada, b_ada.reshape(depth, 1, n))
    return out[:, :bsz]


def _proj_kernel(x_ref, g_ref, sh_ref, sc_ref, wq_ref, wk_ref, wv_ref, wf_ref, bf_ref,
                 wu_ref, wp_ref, ps_ref,
                 q_ref, k_ref, v_ref, dcol_ref, drow_ref, p_ref,
                 ubuf, dcarry, *, tm, n_heads, gc):
    i = pl.program_id(1)
    x = x_ref[0]
    ms = jnp.mean(x * x, axis=-1, keepdims=True)
    y = (x * lax.rsqrt(ms + EPS)) * g_ref[...]
    h = y * (1.0 + sc_ref[0]) + sh_ref[0]
    hb = h.astype(BF16)

    scale = HEAD_DIM ** -0.5
    q_ref[0] = (jnp.dot(hb, wq_ref[...], preferred_element_type=F32) * scale).astype(BF16)
    k_ref[0] = jnp.dot(hb, wk_ref[...], preferred_element_type=F32).astype(BF16)
    v_ref[0] = jnp.dot(hb, wv_ref[...], preferred_element_type=F32).astype(BF16)

    z = jnp.dot(hb, wf_ref[...], preferred_element_type=F32) + bf_ref[...]
    lf = jnp.minimum(z, 0.0) - jnp.log1p(jnp.exp(-jnp.abs(z)))
    r = lax.broadcasted_iota(I32, (tm, tm), 0)
    cidx = lax.broadcasted_iota(I32, (tm, tm), 1)
    tri = (cidx <= r).astype(F32)

    @pl.when(i == 0)
    def _():
        dcarry[...] = jnp.zeros_like(dcarry)
        ubuf[0:POOL_HALO, :] = jnp.zeros((POOL_HALO, ubuf.shape[1]), F32)

    @pl.when(i > 0)
    def _():
        ubuf[0:POOL_HALO, :] = ubuf[tm:tm + POOL_HALO, :]

    dc = jnp.dot(tri, lf, precision=lax.Precision.HIGHEST,
                 preferred_element_type=F32) + dcarry[...]
    dcarry[...] = dc[tm - 1:tm, :]
    dcol_ref[0] = dc
    drow_ref[0] = dc.T[0:drow_ref.shape[1], :]

    u = jnp.dot(hb, wu_ref[...], preferred_element_type=F32)
    ubuf[POOL_HALO:POOL_HALO + tm, :] = u
    pos = i * tm + lax.broadcasted_iota(I32, (tm, 1), 0)
    for g, w in enumerate(POOL_WINDOWS):
        s = ubuf[:, g * gc:(g + 1) * gc]
        step = 1
        while step < w:
            s = s + pltpu.roll(s, step, 0)
            step *= 2
        cnt = jnp.minimum(pos + 1, w).astype(F32)
        pooled = s[POOL_HALO:POOL_HALO + tm, :] / cnt - u[:, g * gc:(g + 1) * gc]
        yg = jnp.dot(pooled.astype(BF16), wp_ref[g], preferred_element_type=F32)
        p_ref[0, :, g * gc:(g + 1) * gc] = (yg * ps_ref[:, g * gc:(g + 1) * gc]).astype(BF16)


def _proj(x, g, sh, sc, wq, wk, wv, wf, bf, wu, wp, ps, *, n_heads):
    bsz, seq, d = x.shape
    da = wq.shape[1]
    dp = wu.shape[1]
    gc = dp // len(POOL_WINDOWS)
    tm = min(512, seq)
    nt = seq // tm
    kern = functools.partial(_proj_kernel, tm=tm, n_heads=n_heads, gc=gc)
    row = lambda b, i: (b, i, 0)
    per_b = lambda b, i: (b, 0, 0)
    const2 = lambda b, i: (0, 0)
    const3 = lambda b, i: (0, 0, 0)
    return pl.pallas_call(
        kern,
        out_shape=(jax.ShapeDtypeStruct((bsz, seq, da), BF16),
                   jax.ShapeDtypeStruct((bsz, seq, da), BF16),
                   jax.ShapeDtypeStruct((bsz, seq, da), BF16),
                   jax.ShapeDtypeStruct((bsz, seq, LANES), F32),
                   jax.ShapeDtypeStruct((bsz, 8, seq), F32),
                   jax.ShapeDtypeStruct((bsz, seq, dp), BF16)),
        grid=(bsz, nt),
        in_specs=[pl.BlockSpec((1, tm, d), row),
                  pl.BlockSpec((1, d), const2),
                  pl.BlockSpec((1, 1, d), per_b),
                  pl.BlockSpec((1, 1, d), per_b),
                  _resident((d, da), const2),
                  _resident((d, da), const2),
                  _resident((d, da), const2),
                  _resident((d, LANES), const2),
                  pl.BlockSpec((1, LANES), const2),
                  _resident((d, dp), const2),
                  _resident(wp.shape, const3),
                  pl.BlockSpec((1, dp), const2)],
        out_specs=(pl.BlockSpec((1, tm, da), row),
                   pl.BlockSpec((1, tm, da), row),
                   pl.BlockSpec((1, tm, da), row),
                   pl.BlockSpec((1, tm, LANES), row),
                   pl.BlockSpec((1, 8, tm), lambda b, i: (b, 0, i)),
                   pl.BlockSpec((1, tm, dp), row)),
        scratch_shapes=[pltpu.VMEM((tm + POOL_HALO, dp), F32),
                        pltpu.VMEM((1, LANES), F32)],
        compiler_params=_cparams(("arbitrary", "arbitrary")),
        name="proj",
    )(x, g, sh, sc, wq, wk, wv, wf, bf, wu, wp, ps)


def _attn_kernel(q_ref, k_ref, v_ref, dcol_ref, drow_ref, o_ref, *, tq, hp):
    hg = pl.program_id(1)
    i = pl.program_id(2)
    lane = lax.broadcasted_iota(I32, (tq, LANES), 1)
    dcol = dcol_ref[0]
    heads = [slice(u * HEAD_DIM, (u + 1) * HEAD_DIM) for u in range(hp)]
    qs = [q_ref[0, :, hs] for hs in heads]
    dqs = [jnp.sum(jnp.where(lane == hg * hp + u, dcol, 0.0), axis=-1, keepdims=True)
           for u in range(hp)]

    def block(j, carry, masked):
        start = pl.multiple_of(j * tq, tq)
        out = []
        for u in range(hp):
            m, l, acc = carry[u]
            ks = k_ref[0, pl.ds(start, tq), heads[u]]
            vs = v_ref[0, pl.ds(start, tq), heads[u]]
            dk = drow_ref[0, pl.ds(hg * hp + u, 1), pl.ds(start, tq)]
            z = lax.dot_general(qs[u], ks, _NT, preferred_element_type=F32) - dk
            if masked:
                rr = lax.broadcasted_iota(I32, (tq, tq), 0)
                cc = lax.broadcasted_iota(I32, (tq, tq), 1)
                z = jnp.where(cc <= rr, z, -jnp.inf)
            m_new = jnp.maximum(m, jnp.max(z, axis=-1, keepdims=True) + dqs[u])
            p = jnp.exp(z - (m_new - dqs[u]))
            alpha = jnp.exp(m - m_new)
            l = alpha * l + jnp.sum(p, axis=-1, keepdims=True)
            acc = alpha * acc + jnp.dot(p.astype(BF16), vs, preferred_element_type=F32)
            out.append((m_new, l, acc))
        return tuple(out)

    init = tuple((jnp.full((tq, 1), -jnp.inf, F32), jnp.zeros((tq, 1), F32),
                  jnp.zeros((tq, HEAD_DIM), F32)) for _ in range(hp))
    carry = lax.fori_loop(0, i, functools.partial(block, masked=False), init)
    carry = block(i, carry, True)
    for u in range(hp):
        _, l, acc = carry[u]
        o_ref[0, :, heads[u]] = (acc / l).astype(o_ref.dtype)


def _attn(q, k, v, dcol, drow, *, n_heads):
    bsz, seq, da = q.shape
    tq = min(512, seq)
    nq = seq // tq
    hp = 2 if n_heads % 2 == 0 else 1
    kern = functools.partial(_attn_kernel, tq=tq, hp=hp)
    return pl.pallas_call(
        kern,
        out_shape=jax.ShapeDtypeStruct((bsz, seq, da), BF16),
        grid=(bsz, n_heads // hp, nq),
        in_specs=[pl.BlockSpec((1, tq, hp * HEAD_DIM), lambda b, h, i: (b, i, h)),
                  pl.BlockSpec((1, seq, hp * HEAD_DIM), lambda b, h, i: (b, 0, h)),
                  pl.BlockSpec((1, seq, hp * HEAD_DIM), lambda b, h, i: (b, 0, h)),
                  pl.BlockSpec((1, tq, LANES), lambda b, h, i: (b, i, 0)),
                  pl.BlockSpec((1, 8, seq), lambda b, h, i: (b, 0, 0))],
        out_specs=pl.BlockSpec((1, tq, hp * HEAD_DIM), lambda b, h, i: (b, i, h)),
        compiler_params=_cparams(("arbitrary", "arbitrary", "arbitrary")),
        name="attn",
    )(q, k, v, dcol, drow)


def _post_kernel(a_ref, p_ref, x_ref, g1_ref, ng_ref, sh_ref, sc_ref, woa_ref, wop_ref,
                 wr1_ref, wr2_ref, rb_ref,
                 x1_ref, h2_ref, idx_ref, wts_ref, rank_ref, cnt_ref,
                 carry, *, tm, n_exp):
    first = jnp.logical_and(pl.program_id(0) == 0, pl.program_id(1) == 0)

    @pl.when(first)
    def _():
        carry[...] = jnp.zeros_like(carry)

    mix = jnp.dot(a_ref[0], woa_ref[...], preferred_element_type=F32)
    mix = mix + jnp.dot(p_ref[0], wop_ref[...], preferred_element_type=F32)
    x1 = x_ref[0] + g1_ref[0] * mix
    x1_ref[0] = x1
    ms = jnp.mean(x1 * x1, axis=-1, keepdims=True)
    y = (x1 * lax.rsqrt(ms + EPS)) * ng_ref[...]
    h2 = y * (1.0 + sc_ref[0]) + sh_ref[0]
    h2_ref[0] = _pack_halves(h2)

    h_hi = h2.astype(BF16)
    h_lo = (h2 - h_hi.astype(F32)).astype(BF16)
    a1 = jnp.dot(h_hi, wr1_ref[...], preferred_element_type=F32)
    a2 = jnp.dot(h_lo, wr2_ref[...], preferred_element_type=F32)
    logits = a1[:, :LANES] + a1[:, LANES:] + a2
    lt = logits.T[0:n_exp, :]
    sc = jax.nn.sigmoid(lt)
    biased = sc + rb_ref[...]

    epg = n_exp // N_EXPERT_GROUPS
    sub = lax.broadcasted_iota(I32, (epg, tm), 0)
    blocks, gscore = [], []
    for g in range(N_EXPERT_GROUPS):
        blk = biased[g * epg:(g + 1) * epg, :]
        m1 = jnp.max(blk, axis=0, keepdims=True)
        f1 = jnp.min(jnp.where(blk == m1, sub, epg), axis=0, keepdims=True)
        m2 = jnp.max(jnp.where(sub == f1, -jnp.inf, blk), axis=0, keepdims=True)
        blocks.append(blk)
        gscore.append(m1 + m2)
    masked = []
    for g in range(N_EXPERT_GROUPS):
        beats = jnp.zeros((1, tm), I32)
        for g2 in range(N_EXPERT_GROUPS):
            if g2 == g:
                continue
            win = (gscore[g2] > gscore[g]) if g2 > g else (gscore[g2] >= gscore[g])
            beats = beats + win.astype(I32)
        masked.append(jnp.where(beats < TOPK_GROUPS, blocks[g], -jnp.inf))
    v = jnp.concatenate(masked, axis=0)

    eio = lax.broadcasted_iota(I32, (n_exp, tm), 0)
    base = carry[...]
    hits, svals = [], []
    sel = jnp.zeros((n_exp, tm), F32)
    for k in range(TOP_K):
        m = jnp.max(v, axis=0, keepdims=True)
        idx = jnp.min(jnp.where(v == m, eio, n_exp), axis=0, keepdims=True)
        hit = eio == idx
        svals.append(jnp.sum(jnp.where(hit, sc, 0.0), axis=0, keepdims=True))
        v = jnp.where(hit, -jnp.inf, v)
        sel = sel + hit.astype(F32)
        hits.append(hit)
        idx_ref[k:k + 1, :] = idx
    total = svals[0]
    for k in range(1, TOP_K):
        total = total + svals[k]
    for k in range(TOP_K):
        wts_ref[k:k + 1, :] = svals[k] / total * ROUTED_SCALE

    rr = lax.broadcasted_iota(I32, (tm, tm), 0)
    cc = lax.broadcasted_iota(I32, (tm, tm), 1)
    upper = (rr < cc).astype(BF16)
    before = jnp.dot(sel.astype(BF16), upper, preferred_element_type=F32).astype(I32) + base
    for k in range(TOP_K):
        rank_ref[k:k + 1, :] = jnp.sum(jnp.where(hits[k], before, 0), axis=0, keepdims=True)
    new = base + jnp.sum(sel, axis=1, keepdims=True).astype(I32)
    carry[...] = new
    cnt_ref[...] = new


def _post(a, p, x, g1, ng, sh, sc, woa, wop, wr1, wr2, rb):
    bsz, seq, d = x.shape
    da = a.shape[2]
    dp = p.shape[2]
    n_exp = rb.shape[0]
    tm = min(512, seq)
    nt = seq // tm
    tokens = bsz * seq
    kern = functools.partial(_post_kernel, tm=tm, n_exp=n_exp)
    row = lambda b, i: (b, i, 0)
    per_b = lambda b, i: (b, 0, 0)
    const2 = lambda b, i: (0, 0)
    tok = lambda b, i: (0, b * nt + i)
    return pl.pallas_call(
        kern,
        out_shape=(jax.ShapeDtypeStruct((bsz, seq, d), F32),
                   jax.ShapeDtypeStruct((bsz, seq, d // 2), U32),
                   jax.ShapeDtypeStruct((TOP_K, tokens), I32),
                   jax.ShapeDtypeStruct((TOP_K, tokens), F32),
                   jax.ShapeDtypeStruct((TOP_K, tokens), I32),
                   jax.ShapeDtypeStruct((n_exp, 1), I32)),
        grid=(bsz, nt),
        in_specs=[pl.BlockSpec((1, tm, da), row),
                  pl.BlockSpec((1, tm, dp), row),
                  pl.BlockSpec((1, tm, d), row),
                  pl.BlockSpec((1, 1, d), per_b),
                  pl.BlockSpec((1, d), const2),
                  pl.BlockSpec((1, 1, d), per_b),
                  pl.BlockSpec((1, 1, d), per_b),
                  _resident((da, d), const2),
                  _resident((dp, d), const2),
                  _resident((d, 2 * LANES), const2),
                  _resident((d, LANES), const2),
                  pl.BlockSpec((n_exp, 1), const2)],
        out_specs=(pl.BlockSpec((1, tm, d), row),
                   pl.BlockSpec((1, tm, d // 2), row),
                   pl.BlockSpec((TOP_K, tm), tok),
                   pl.BlockSpec((TOP_K, tm), tok),
                   pl.BlockSpec((TOP_K, tm), tok),
                   pl.BlockSpec((n_exp, 1), const2)),
        scratch_shapes=[pltpu.VMEM((n_exp, 1), I32)],
        compiler_params=_cparams(("arbitrary", "arbitrary")),
        name="post",
    )(a, p, x, g1, ng, sh, sc, woa, wop, wr1, wr2, rb)


def _row_copy(src, s, dst, t, sem):
    return pltpu.make_async_copy(src.at[pl.ds(s, 1)], dst.at[pl.ds(t, 1)], sem)


def _dispatch_kernel(pstart_ref, cnt_ref, idx_ref, rank_ref, h_ref, xs_ref, pos_ref,
                     zrow, sem, zsem, *, td, n_exp):
    step = pl.program_id(0)

    @pl.when(step == 0)
    def _():
        zrow[...] = jnp.zeros_like(zrow)

        def per_expert(e, total):
            lo = pstart_ref[e] + cnt_ref[e]
            hi = pstart_ref[e + 1]

            def fill(r, c):
                _row_copy(zrow, 0, xs_ref, r, zsem).start()
                return c

            lax.fori_loop(lo, hi, fill, 0)
            return total + (hi - lo)

        npad = lax.fori_loop(0, n_exp, per_expert, 0)

        def drain(r, c):
            _row_copy(zrow, 0, xs_ref, 0, zsem).wait()
            return c

        lax.fori_loop(0, npad, drain, 0)

    def per_token(t, c):
        for k in range(TOP_K):
            pos = pstart_ref[idx_ref[k, t]] + rank_ref[k, t]
            pos_ref[k, t] = pos
            _row_copy(h_ref, t, xs_ref, pos, sem).start(priority=k % 2)
        return c

    lax.fori_loop(0, td, per_token, 0, unroll=2)

    for k in range(TOP_K):
        pltpu.make_async_copy(h_ref, xs_ref.at[pl.ds(0, td)], sem).wait()


def _dispatch(pstart, counts, idx_t, rank_t, h2, n_rows):
    tokens, d = h2.shape
    n_exp = counts.shape[0]
    td = min(512, tokens)
    kern = functools.partial(_dispatch_kernel, td=td, n_exp=n_exp)
    smem_blk = pl.BlockSpec((TOP_K, td), lambda i, *_: (0, i), memory_space=pltpu.SMEM)
    return pl.pallas_call(
        kern,
        out_shape=(jax.ShapeDtypeStruct((n_rows, d), h2.dtype),
                   jax.ShapeDtypeStruct((TOP_K, tokens), I32)),
        grid_spec=pltpu.PrefetchScalarGridSpec(
            num_scalar_prefetch=2,
            grid=(tokens // td,),
            in_specs=[smem_blk, smem_blk, pl.BlockSpec((td, d), lambda i, *_: (i, 0))],
            out_specs=(pl.BlockSpec(memory_space=pl.ANY), smem_blk),
            scratch_shapes=[pltpu.VMEM((8, d), h2.dtype),
                            pltpu.SemaphoreType.DMA, pltpu.SemaphoreType.DMA]),
        compiler_params=_cparams(("arbitrary",)),
        name="dispatch",
    )(pstart, counts, idx_t, rank_t, h2)


def _experts_kernel(bexp_ref, nused_ref, x_ref, w1_ref, w3_ref, w2_ref, y_ref, wb1, wb3, wb2):
    b = pl.program_id(0)
    live = b < nused_ref[0]
    changed = jnp.logical_or(b == 0, bexp_ref[b] != bexp_ref[jnp.maximum(b - 1, 0)])

    @pl.when(jnp.logical_and(live, changed))
    def _():
        wb1[...] = w1_ref[0, 0].astype(BF16)
        wb3[...] = w3_ref[0, 0].astype(BF16)
        wb2[...] = w2_ref[0, 0].astype(BF16)

    @pl.when(live)
    def _():
        half = x_ref.shape[1]
        lo, hi = _unpack_halves(x_ref[...])
        lo = lo.astype(BF16)
        hi = hi.astype(BF16)
        h1 = jnp.dot(lo, wb1[0:half, :], preferred_element_type=F32)
        h1 = h1 + jnp.dot(hi, wb1[half:, :], preferred_element_type=F32)
        h3 = jnp.dot(lo, wb3[0:half, :], preferred_element_type=F32)
        h3 = h3 + jnp.dot(hi, wb3[half:, :], preferred_element_type=F32)
        act = (h1 * jax.nn.sigmoid(h1) * h3).astype(BF16)
        y_ref[...] = _pack_halves(jnp.dot(act, wb2[...], preferred_element_type=F32))


def _experts(block_exp, nused, xs, w1, w3, w2, layer):
    n_rows, half = xs.shape
    d = 2 * half
    de = w1.shape[3]
    nb = n_rows // EXPERT_BLOCK
    blk = lambda b, be, nu: (jnp.minimum(b, nu[0] - 1), 0)
    wsel = lambda b, be, nu: (layer, be[jnp.minimum(b, nu[0] - 1)], 0, 0)
    return pl.pallas_call(
        _experts_kernel,
        out_shape=jax.ShapeDtypeStruct((n_rows, half), U32),
        grid_spec=pltpu.PrefetchScalarGridSpec(
            num_scalar_prefetch=2,
            grid=(nb,),
            in_specs=[pl.BlockSpec((EXPERT_BLOCK, half), blk),
                      pl.BlockSpec((1, 1, d, de), wsel),
                      pl.BlockSpec((1, 1, d, de), wsel),
                      pl.BlockSpec((1, 1, de, d), wsel)],
            out_specs=pl.BlockSpec((EXPERT_BLOCK, half), blk),
            scratch_shapes=[pltpu.VMEM((d, de), BF16), pltpu.VMEM((d, de), BF16),
                            pltpu.VMEM((de, d), BF16)]),
        compiler_params=_cparams(("arbitrary",)),
        name="experts",
    )(block_exp, nused, xs, w1, w3, w2)


def _combine_kernel(pos_ref, y_ref, wts_ref, h_ref, x1_ref, g2_ref, w1_ref, w3_ref, w2_ref,
                    o_ref, buf, sem, *, tc):
    def per_token(t, c):
        for k in range(TOP_K):
            pltpu.make_async_copy(y_ref.at[pl.ds(pos_ref[k, t], 1)],
                                  buf.at[k, pl.ds(t, 1)], sem).start(priority=k % 2)
        return c

    lax.fori_loop(0, tc, per_token, 0, unroll=2)

    half = h_ref.shape[2]
    lo, hi = _unpack_halves(h_ref[0])
    lo = lo.astype(BF16)
    hi = hi.astype(BF16)
    h1 = jnp.dot(lo, w1_ref[0:half, :], preferred_element_type=F32)
    h1 = h1 + jnp.dot(hi, w1_ref[half:, :], preferred_element_type=F32)
    h3 = jnp.dot(lo, w3_ref[0:half, :], preferred_element_type=F32)
    h3 = h3 + jnp.dot(hi, w3_ref[half:, :], preferred_element_type=F32)
    act = (h1 * jax.nn.sigmoid(h1) * h3).astype(BF16)
    shared = jnp.dot(act, w2_ref[...], preferred_element_type=F32)

    for k in range(TOP_K):
        pltpu.make_async_copy(y_ref.at[pl.ds(0, tc)], buf.at[k], sem).wait()

    w = wts_ref[...]
    r_lo, r_hi = _unpack_halves(buf[0])
    r_lo = r_lo * w[:, 0:1]
    r_hi = r_hi * w[:, 0:1]
    for k in range(1, TOP_K):
        y_lo, y_hi = _unpack_halves(buf[k])
        r_lo = r_lo + y_lo * w[:, k:k + 1]
        r_hi = r_hi + y_hi * w[:, k:k + 1]
    g2 = g2_ref[0]
    o_ref[0, :, 0:half] = x1_ref[0, :, 0:half] + g2[:, 0:half] * (r_lo + shared[:, 0:half])
    o_ref[0, :, half:] = x1_ref[0, :, half:] + g2[:, half:] * (r_hi + shared[:, half:])


def _combine(pos_t, y, wts_tok, h2, x1, g2, w1s, w3s, w2s):
    bsz, seq, d = x1.shape
    half = d // 2
    de = w1s.shape[1]
    tc = min(256, seq)
    nt = seq // tc
    kern = functools.partial(_combine_kernel, tc=tc)
    row = lambda b, i: (b, i, 0)
    const2 = lambda b, i: (0, 0)
    return pl.pallas_call(
        kern,
        out_shape=jax.ShapeDtypeStruct((bsz, seq, d), F32),
        grid=(bsz, nt),
        in_specs=[pl.BlockSpec((TOP_K, tc), lambda b, i: (0, b * nt + i),
                               memory_space=pltpu.SMEM),
                  pl.BlockSpec(memory_space=pl.ANY),
                  pl.BlockSpec((tc, TOP_K), lambda b, i: (b * nt + i, 0)),
                  pl.BlockSpec((1, tc, half), row),
                  pl.BlockSpec((1, tc, d), row),
                  pl.BlockSpec((1, 1, d), lambda b, i: (b, 0, 0)),
                  _resident((d, de), const2),
                  _resident((d, de), const2),
                  _resident((de, d), const2)],
        out_specs=pl.BlockSpec((1, tc, d), row),
        scratch_shapes=[pltpu.VMEM((TOP_K, tc, half), U32), pltpu.SemaphoreType.DMA],
        compiler_params=_cparams(("arbitrary", "arbitrary")),
        name="combine",
    )(pos_t, y, wts_tok, h2, x1, g2, w1s, w3s, w2s)


def _final_kernel(x_ref, g_ref, o_ref):
    x = x_ref[0]
    ms = jnp.mean(x * x, axis=-1, keepdims=True)
    o_ref[0] = (x * lax.rsqrt(ms + EPS)) * g_ref[...]


def _final_norm(x, g):
    bsz, seq, d = x.shape
    tm = min(512, seq)
    row = lambda b, i: (b, i, 0)
    return pl.pallas_call(
        _final_kernel,
        out_shape=jax.ShapeDtypeStruct(x.shape, F32),
        grid=(bsz, seq // tm),
        in_specs=[pl.BlockSpec((1, tm, d), row), pl.BlockSpec((1, d), lambda b, i: (0, 0))],
        out_specs=pl.BlockSpec((1, tm, d), row),
        compiler_params=_cparams(("arbitrary", "arbitrary")),
        name="final_norm",
    )(x, g)


def _pad_lanes(w, n):
    return jnp.pad(w, ((0, 0), (0, n - w.shape[1])))


def kernel(x, c, norm1_g, norm2_g, w_ada, b_ada, w_in, b_f, w_pool, pool_scale, w_o,
           w_router, router_bias, w1, w3, w2, w1_shared, w3_shared, w2_shared, final_g):
    bsz, seq, d = x.shape
    depth = w_ada.shape[0]
    n_heads = b_f.shape[1]
    da = n_heads * HEAD_DIM
    dp = d - da
    n_exp = w_router.shape[2]
    tokens = bsz * seq
    n_assign = tokens * TOP_K
    n_blocks = -(-n_assign // EXPERT_BLOCK) + n_exp
    n_rows = n_blocks * EXPERT_BLOCK

    mod = _ada(c, w_ada, b_ada)
    mod = mod.reshape(depth, bsz, N_MOD, 1, d)

    for l in range(depth):
        sh1, sc1, g1, sh2, sc2, g2 = (mod[l, :, j] for j in range(N_MOD))
        wl = w_in[l]
        wq = wl[:, :da].astype(BF16)
        wk = wl[:, da:2 * da].astype(BF16)
        wv = wl[:, 2 * da:3 * da].astype(BF16)
        wf = _pad_lanes(wl[:, 3 * da:3 * da + n_heads], LANES).astype(BF16)
        wu = wl[:, 3 * da + n_heads:].astype(BF16)
        bf = _pad_lanes(b_f[l][None, :], LANES)
        q, k, v, dcol, drow, p = _proj(
            x, norm1_g[l][None, :], sh1, sc1, wq, wk, wv, wf, bf, wu,
            w_pool[l].astype(BF16), pool_scale[l][None, :], n_heads=n_heads)
        a = _attn(q, k, v, dcol, drow, n_heads=n_heads)

        wr = _pad_lanes(w_router[l], LANES)
        wr_hi = wr.astype(BF16)
        wr_lo = (wr - wr_hi.astype(F32)).astype(BF16)
        wo = w_o[l].astype(BF16)
        x1, h2, idx_t, wts_t, rank_t, counts = _post(
            a, p, x, g1, norm2_g[l][None, :], sh2, sc2, wo[:da], wo[da:],
            jnp.concatenate([wr_hi, wr_lo], axis=1), wr_hi, router_bias[l][:, None])

        counts = counts[:, 0]
        padded = ((counts + EXPERT_BLOCK - 1) // EXPERT_BLOCK) * EXPERT_BLOCK
        cum = jnp.cumsum(padded)
        pstart = jnp.concatenate([jnp.zeros((1,), I32), cum]).astype(I32)
        nused = (cum[-1:] // EXPERT_BLOCK).astype(I32)
        block_starts = jnp.arange(n_blocks, dtype=I32) * EXPERT_BLOCK
        block_exp = jnp.minimum(
            jnp.sum((cum[None, :] <= block_starts[:, None]).astype(I32), axis=1), n_exp - 1)

        h2f = h2.reshape(tokens, d // 2)
        xs, pos_t = _dispatch(pstart, counts, idx_t, rank_t, h2f, n_rows)
        y = _experts(block_exp, nused, xs, w1, w3, w2, l)
        x = _combine(pos_t, y, wts_t.T, h2, x1, g2, w1_shared[l].astype(BF16),
                     w3_shared[l].astype(BF16), w2_shared[l].astype(BF16))
    return _final_norm(x, final_g[None, :])
```

```python
import functools

import jax
import jax.numpy as jnp
from jax import lax
from jax.experimental import pallas as pl
from jax.experimental.pallas import tpu as pltpu

F32 = jnp.float32
BF16 = jnp.bfloat16
I32 = jnp.int32
U32 = jnp.uint32

EPS = 1e-6
HEAD_DIM = 128
LANES = 128
POOL_WINDOWS = (2, 4, 8, 16)
POOL_HALO = 16
TOP_K = 8
N_EXPERT_GROUPS = 8
TOPK_GROUPS = 4
ROUTED_SCALE = 2.5
EXPERT_BLOCK = 512
N_MOD = 6
VMEM_LIMIT = 56 * 1024 * 1024

_NT = (((1,), (1,)), ((), ()))


def _cparams(sem):
    return pltpu.CompilerParams(dimension_semantics=sem, vmem_limit_bytes=VMEM_LIMIT)


PAIR = 2 * LANES


def _pack_pair(lo, hi):
    lo = lax.bitcast_convert_type(lo.astype(BF16).astype(F32), U32)
    hi = lax.bitcast_convert_type(hi.astype(BF16).astype(F32), U32)
    return hi | (lo >> 16)


def _unpack_pair(p):
    lo = lax.bitcast_convert_type(p << 16, F32)
    hi = lax.bitcast_convert_type(p & jnp.uint32(0xFFFF0000), F32)
    return lo, hi


def _store_rows(ref, x, m):
    ns = x.shape[1] // PAIR
    for s in range(ns):
        ref[pl.ds(s, m, stride=ns), :] = _pack_pair(x[:, s * PAIR:s * PAIR + LANES],
                                                    x[:, s * PAIR + LANES:(s + 1) * PAIR])


def _load_chunk_bf16(ref, s, m, ns):
    lo, hi = _unpack_pair(ref[pl.ds(s, m, stride=ns), :])
    return jnp.concatenate([lo.astype(BF16), hi.astype(BF16)], axis=1)


def _resident(shape, index_map):
    return pl.BlockSpec(shape, index_map, pipeline_mode=pl.Buffered(1))


def _ada_kernel(c_ref, w_ref, b_ref, o_ref):
    c = c_ref[...]
    ca = c * jax.nn.sigmoid(c)
    o_ref[0] = jnp.dot(ca, w_ref[0], precision=lax.Precision.HIGHEST,
                       preferred_element_type=F32) + b_ref[0]


def _ada(c, w_ada, b_ada):
    depth, d, n = w_ada.shape
    bsz = c.shape[0]
    rows = 8
    cp = jnp.zeros((rows, d), F32).at[:bsz].set(c)
    tn = 512
    out = pl.pallas_call(
        _ada_kernel,
        out_shape=jax.ShapeDtypeStruct((depth, rows, n), F32),
        grid=(depth, n // tn),
        in_specs=[pl.BlockSpec((rows, d), lambda l, j: (0, 0)),
                  pl.BlockSpec((1, d, tn), lambda l, j: (l, 0, j)),
                  pl.BlockSpec((1, 1, tn), lambda l, j: (l, 0, j))],
        out_specs=pl.BlockSpec((1, rows, tn), lambda l, j: (l, 0, j)),
        compiler_params=_cparams(("arbitrary", "arbitrary")),
        name="ada",
    )(cp, w_ada, b_ada.reshape(depth, 1, n))
    return out[:, :bsz]


def _proj_kernel(x_ref, g_ref, sh_ref, sc_ref, wq_ref, wk_ref, wv_ref, wf_ref, bf_ref,
                 wu_ref, wp_ref, ps_ref,
                 q_ref, k_ref, v_ref, dcol_ref, drow_ref, p_ref,
                 ubuf, dcarry, *, tm, n_heads, gc):
    i = pl.program_id(1)
    x = x_ref[0]
    ms = jnp.mean(x * x, axis=-1, keepdims=True)
    y = (x * lax.rsqrt(ms + EPS)) * g_ref[...]
    h = y * (1.0 + sc_ref[0]) + sh_ref[0]
    hb = h.astype(BF16)

    scale = HEAD_DIM ** -0.5
    q_ref[0] = (jnp.dot(hb, wq_ref[...], preferred_element_type=F32) * scale).astype(BF16)
    k_ref[0] = jnp.dot(hb, wk_ref[...], preferred_element_type=F32).astype(BF16)
    v_ref[0] = jnp.dot(hb, wv_ref[...], preferred_element_type=F32).astype(BF16)

    z = jnp.dot(hb, wf_ref[...], preferred_element_type=F32) + bf_ref[...]
    lf = jnp.minimum(z, 0.0) - jnp.log1p(jnp.exp(-jnp.abs(z)))
    r = lax.broadcasted_iota(I32, (tm, tm), 0)
    cidx = lax.broadcasted_iota(I32, (tm, tm), 1)
    tri = (cidx <= r).astype(F32)

    @pl.when(i == 0)
    def _():
        dcarry[...] = jnp.zeros_like(dcarry)
        ubuf[0:POOL_HALO, :] = jnp.zeros((POOL_HALO, ubuf.shape[1]), F32)

    @pl.when(i > 0)
    def _():
        ubuf[0:POOL_HALO, :] = ubuf[tm:tm + POOL_HALO, :]

    dc = jnp.dot(tri, lf, precision=lax.Precision.HIGHEST,
                 preferred_element_type=F32) + dcarry[...]
    dcarry[...] = dc[tm - 1:tm, :]
    dcol_ref[0] = dc
    drow_ref[0] = dc.T[0:drow_ref.shape[1], :]

    u = jnp.dot(hb, wu_ref[...], preferred_element_type=F32)
    ubuf[POOL_HALO:POOL_HALO + tm, :] = u
    pos = i * tm + lax.broadcasted_iota(I32, (tm, 1), 0)
    for g, w in enumerate(POOL_WINDOWS):
        s = ubuf[:, g * gc:(g + 1) * gc]
        step = 1
        while step < w:
            s = s + pltpu.roll(s, step, 0)
            step *= 2
        cnt = jnp.minimum(pos + 1, w).astype(F32)
        pooled = s[POOL_HALO:POOL_HALO + tm, :] / cnt - u[:, g * gc:(g + 1) * gc]
        yg = jnp.dot(pooled.astype(BF16), wp_ref[g], preferred_element_type=F32)
        p_ref[0, :, g * gc:(g + 1) * gc] = (yg * ps_ref[:, g * gc:(g + 1) * gc]).astype(BF16)


def _proj(x, g, sh, sc, wq, wk, wv, wf, bf, wu, wp, ps, *, n_heads):
    bsz, seq, d = x.shape
    da = wq.shape[1]
    dp = wu.shape[1]
    gc = dp // len(POOL_WINDOWS)
    tm = min(512, seq)
    nt = seq // tm
    kern = functools.partial(_proj_kernel, tm=tm, n_heads=n_heads, gc=gc)
    row = lambda b, i: (b, i, 0)
    per_b = lambda b, i: (b, 0, 0)
    const2 = lambda b, i: (0, 0)
    const3 = lambda b, i: (0, 0, 0)
    return pl.pallas_call(
        kern,
        out_shape=(jax.ShapeDtypeStruct((bsz, seq, da), BF16),
                   jax.ShapeDtypeStruct((bsz, seq, da), BF16),
                   jax.ShapeDtypeStruct((bsz, seq, da), BF16),
                   jax.ShapeDtypeStruct((bsz, seq, LANES), F32),
                   jax.ShapeDtypeStruct((bsz, 8, seq), F32),
                   jax.ShapeDtypeStruct((bsz, seq, dp), BF16)),
        grid=(bsz, nt),
        in_specs=[pl.BlockSpec((1, tm, d), row),
                  pl.BlockSpec((1, d), const2),
                  pl.BlockSpec((1, 1, d), per_b),
                  pl.BlockSpec((1, 1, d), per_b),
                  _resident((d, da), const2),
                  _resident((d, da), const2),
                  _resident((d, da), const2),
                  _resident((d, LANES), const2),
                  pl.BlockSpec((1, LANES), const2),
                  _resident((d, dp), const2),
                  _resident(wp.shape, const3),
                  pl.BlockSpec((1, dp), const2)],
        out_specs=(pl.BlockSpec((1, tm, da), row),
                   pl.BlockSpec((1, tm, da), row),
                   pl.BlockSpec((1, tm, da), row),
                   pl.BlockSpec((1, tm, LANES), row),
                   pl.BlockSpec((1, 8, tm), lambda b, i: (b, 0, i)),
                   pl.BlockSpec((1, tm, dp), row)),
        scratch_shapes=[pltpu.VMEM((tm + POOL_HALO, dp), F32),
                        pltpu.VMEM((1, LANES), F32)],
        compiler_params=_cparams(("arbitrary", "arbitrary")),
        name="proj",
    )(x, g, sh, sc, wq, wk, wv, wf, bf, wu, wp, ps)


def _attn_kernel(q_ref, k_ref, v_ref, dcol_ref, drow_ref, o_ref, *, tq, hp):
    hg = pl.program_id(1)
    i = pl.program_id(2)
    lane = lax.broadcasted_iota(I32, (tq, LANES), 1)
    dcol = dcol_ref[0]
    heads = [slice(u * HEAD_DIM, (u + 1) * HEAD_DIM) for u in range(hp)]
    qs = [q_ref[0, :, hs] for hs in heads]
    dqs = [jnp.sum(jnp.where(lane == hg * hp + u, dcol, 0.0), axis=-1, keepdims=True)
           for u in range(hp)]

    def block(j, carry, masked):
        start = pl.multiple_of(j * tq, tq)
        out = []
        for u in range(hp):
            m, l, acc = carry[u]
            ks = k_ref[0, pl.ds(start, tq), heads[u]]
            vs = v_ref[0, pl.ds(start, tq), heads[u]]
            dk = drow_ref[0, pl.ds(hg * hp + u, 1), pl.ds(start, tq)]
            z = lax.dot_general(qs[u], ks, _NT, preferred_element_type=F32) - dk
            if masked:
                rr = lax.broadcasted_iota(I32, (tq, tq), 0)
                cc = lax.broadcasted_iota(I32, (tq, tq), 1)
                z = jnp.where(cc <= rr, z, -jnp.inf)
            m_new = jnp.maximum(m, jnp.max(z, axis=-1, keepdims=True) + dqs[u])
            p = jnp.exp(z - (m_new - dqs[u]))
            alpha = jnp.exp(m - m_new)
            l = alpha * l + jnp.sum(p, axis=-1, keepdims=True)
            acc = alpha * acc + jnp.dot(p.astype(BF16), vs, preferred_element_type=F32)
            out.append((m_new, l, acc))
        return tuple(out)

    init = tuple((jnp.full((tq, 1), -jnp.inf, F32), jnp.zeros((tq, 1), F32),
                  jnp.zeros((tq, HEAD_DIM), F32)) for _ in range(hp))
    carry = lax.fori_loop(0, i, functools.partial(block, masked=False), init)
    carry = block(i, carry, True)
    for u in range(hp):
        _, l, acc = carry[u]
        o_ref[0, :, heads[u]] = (acc / l).astype(o_ref.dtype)


def _attn(q, k, v, dcol, drow, *, n_heads):
    bsz, seq, da = q.shape
    tq = min(512, seq)
    nq = seq // tq
    hp = 2 if n_heads % 2 == 0 else 1
    kern = functools.partial(_attn_kernel, tq=tq, hp=hp)
    return pl.pallas_call(
        kern,
        out_shape=jax.ShapeDtypeStruct((bsz, seq, da), BF16),
        grid=(bsz, n_heads // hp, nq),
        in_specs=[pl.BlockSpec((1, tq, hp * HEAD_DIM), lambda b, h, i: (b, i, h)),
                  pl.BlockSpec((1, seq, hp * HEAD_DIM), lambda b, h, i: (b, 0, h)),
                  pl.BlockSpec((1, seq, hp * HEAD_DIM), lambda b, h, i: (b, 0, h)),
                  pl.BlockSpec((1, tq, LANES), lambda b, h, i: (b, i, 0)),
                  pl.BlockSpec((1, 8, seq), lambda b, h, i: (b, 0, 0))],
        out_specs=pl.BlockSpec((1, tq, hp * HEAD_DIM), lambda b, h, i: (b, i, h)),
        compiler_params=_cparams(("arbitrary", "arbitrary", "arbitrary")),
        name="attn",
    )(q, k, v, dcol, drow)


def _post_kernel(a_ref, p_ref, x_ref, g1_ref, ng_ref, sh_ref, sc_ref, woa_ref, wop_ref,
                 wr1_ref, wr2_ref, rb_ref,
                 x1_ref, h2_ref, idx_ref, wts_ref, rank_ref, cnt_ref,
                 carry, *, tm, n_exp):
    first = jnp.logical_and(pl.program_id(0) == 0, pl.program_id(1) == 0)

    @pl.when(first)
    def _():
        carry[...] = jnp.zeros_like(carry)

    mix = jnp.dot(a_ref[0], woa_ref[...], preferred_element_type=F32)
    mix = mix + jnp.dot(p_ref[0], wop_ref[...], preferred_element_type=F32)
    x1 = x_ref[0] + g1_ref[0] * mix
    x1_ref[0] = x1
    ms = jnp.mean(x1 * x1, axis=-1, keepdims=True)
    y = (x1 * lax.rsqrt(ms + EPS)) * ng_ref[...]
    h2 = y * (1.0 + sc_ref[0]) + sh_ref[0]
    _store_rows(h2_ref, h2, tm)

    h_hi = h2.astype(BF16)
    h_lo = (h2 - h_hi.astype(F32)).astype(BF16)
    a1 = jnp.dot(h_hi, wr1_ref[...], preferred_element_type=F32)
    a2 = jnp.dot(h_lo, wr2_ref[...], preferred_element_type=F32)
    logits = a1[:, :LANES] + a1[:, LANES:] + a2
    lt = logits.T[0:n_exp, :]
    sc = jax.nn.sigmoid(lt)
    biased = sc + rb_ref[...]

    epg = n_exp // N_EXPERT_GROUPS
    sub = lax.broadcasted_iota(I32, (epg, tm), 0)
    blocks, gscore = [], []
    for g in range(N_EXPERT_GROUPS):
        blk = biased[g * epg:(g + 1) * epg, :]
        m1 = jnp.max(blk, axis=0, keepdims=True)
        f1 = jnp.min(jnp.where(blk == m1, sub, epg), axis=0, keepdims=True)
        m2 = jnp.max(jnp.where(sub == f1, -jnp.inf, blk), axis=0, keepdims=True)
        blocks.append(blk)
        gscore.append(m1 + m2)
    masked = []
    for g in range(N_EXPERT_GROUPS):
        beats = jnp.zeros((1, tm), I32)
        for g2 in range(N_EXPERT_GROUPS):
            if g2 == g:
                continue
            win = (gscore[g2] > gscore[g]) if g2 > g else (gscore[g2] >= gscore[g])
            beats = beats + win.astype(I32)
        masked.append(jnp.where(beats < TOPK_GROUPS, blocks[g], -jnp.inf))
    v = jnp.concatenate(masked, axis=0)

    eio = lax.broadcasted_iota(I32, (n_exp, tm), 0)
    base = carry[...]
    hits, svals = [], []
    sel = jnp.zeros((n_exp, tm), F32)
    for k in range(TOP_K):
        m = jnp.max(v, axis=0, keepdims=True)
        idx = jnp.min(jnp.where(v == m, eio, n_exp), axis=0, keepdims=True)
        hit = eio == idx
        svals.append(jnp.sum(jnp.where(hit, sc, 0.0), axis=0, keepdims=True))
        v = jnp.where(hit, -jnp.inf, v)
        sel = sel + hit.astype(F32)
        hits.append(hit)
        idx_ref[k:k + 1, :] = idx
    total = svals[0]
    for k in range(1, TOP_K):
        total = total + svals[k]
    for k in range(TOP_K):
        wts_ref[k:k + 1, :] = svals[k] / total * ROUTED_SCALE

    rr = lax.broadcasted_iota(I32, (tm, tm), 0)
    cc = lax.broadcasted_iota(I32, (tm, tm), 1)
    upper = (rr < cc).astype(BF16)
    before = jnp.dot(sel.astype(BF16), upper, preferred_element_type=F32).astype(I32) + base
    for k in range(TOP_K):
        rank_ref[k:k + 1, :] = jnp.sum(jnp.where(hits[k], before, 0), axis=0, keepdims=True)
    new = base + jnp.sum(sel, axis=1, keepdims=True).astype(I32)
    carry[...] = new
    cnt_ref[...] = new


def _post(a, p, x, g1, ng, sh, sc, woa, wop, wr1, wr2, rb):
    bsz, seq, d = x.shape
    da = a.shape[2]
    dp = p.shape[2]
    n_exp = rb.shape[0]
    tm = min(512, seq)
    nt = seq // tm
    tokens = bsz * seq
    kern = functools.partial(_post_kernel, tm=tm, n_exp=n_exp)
    row = lambda b, i: (b, i, 0)
    per_b = lambda b, i: (b, 0, 0)
    const2 = lambda b, i: (0, 0)
    tok = lambda b, i: (0, b * nt + i)
    return pl.pallas_call(
        kern,
        out_shape=(jax.ShapeDtypeStruct((bsz, seq, d), F32),
                   jax.ShapeDtypeStruct((tokens * (d // PAIR), LANES), U32),
                   jax.ShapeDtypeStruct((TOP_K, tokens), I32),
                   jax.ShapeDtypeStruct((TOP_K, tokens), F32),
                   jax.ShapeDtypeStruct((TOP_K, tokens), I32),
                   jax.ShapeDtypeStruct((n_exp, 1), I32)),
        grid=(bsz, nt),
        in_specs=[pl.BlockSpec((1, tm, da), row),
                  pl.BlockSpec((1, tm, dp), row),
                  pl.BlockSpec((1, tm, d), row),
                  pl.BlockSpec((1, 1, d), per_b),
                  pl.BlockSpec((1, d), const2),
                  pl.BlockSpec((1, 1, d), per_b),
                  pl.BlockSpec((1, 1, d), per_b),
                  _resident((da, d), const2),
                  _resident((dp, d), const2),
                  _resident((d, 2 * LANES), const2),
                  _resident((d, LANES), const2),
                  pl.BlockSpec((n_exp, 1), const2)],
        out_specs=(pl.BlockSpec((1, tm, d), row),
                   pl.BlockSpec((tm * (d // PAIR), LANES), lambda b, i: (b * nt + i, 0)),
                   pl.BlockSpec((TOP_K, tm), tok),
                   pl.BlockSpec((TOP_K, tm), tok),
                   pl.BlockSpec((TOP_K, tm), tok),
                   pl.BlockSpec((n_exp, 1), const2)),
        scratch_shapes=[pltpu.VMEM((n_exp, 1), I32)],
        compiler_params=_cparams(("arbitrary", "arbitrary")),
        name="post",
    )(a, p, x, g1, ng, sh, sc, woa, wop, wr1, wr2, rb)


def _row_tile(ref, t, ns):
    start = t * ns if isinstance(t, int) else pl.multiple_of(t * ns, ns)
    return ref.at[pl.ds(start, ns)]


def _row_copy(src, s, dst, t, sem, ns):
    return pltpu.make_async_copy(_row_tile(src, s, ns), _row_tile(dst, t, ns), sem)


def _dispatch_kernel(pstart_ref, cnt_ref, idx_ref, rank_ref, h_ref, xs_ref, pos_ref,
                     zrow, sem, zsem, *, td, n_exp, ns):
    step = pl.program_id(0)

    @pl.when(step == 0)
    def _():
        zrow[...] = jnp.zeros_like(zrow)

        def per_expert(e, total):
            lo = pstart_ref[e] + cnt_ref[e]
            hi = pstart_ref[e + 1]

            def fill(r, c):
                _row_copy(zrow, 0, xs_ref, r, zsem, ns).start()
                return c

            lax.fori_loop(lo, hi, fill, 0)
            return total + (hi - lo)

        npad = lax.fori_loop(0, n_exp, per_expert, 0)

        def drain(r, c):
            _row_copy(zrow, 0, xs_ref, 0, zsem, ns).wait()
            return c

        lax.fori_loop(0, npad, drain, 0)

    def per_token(t, c):
        for k in range(TOP_K):
            pos = pstart_ref[idx_ref[k, t]] + rank_ref[k, t]
            pos_ref[k, t] = pos
            _row_copy(h_ref, t, xs_ref, pos, sem, ns).start(priority=k % 2)
        return c

    lax.fori_loop(0, td, per_token, 0, unroll=2)

    for k in range(TOP_K):
        pltpu.make_async_copy(h_ref, xs_ref.at[pl.ds(0, td * ns)], sem).wait()


def _dispatch(pstart, counts, idx_t, rank_t, h2, n_rows, ns):
    tokens = h2.shape[0] // ns
    n_exp = counts.shape[0]
    td = min(512, tokens)
    kern = functools.partial(_dispatch_kernel, td=td, n_exp=n_exp, ns=ns)
    smem_blk = pl.BlockSpec((TOP_K, td), lambda i, *_: (0, i), memory_space=pltpu.SMEM)
    return pl.pallas_call(
        kern,
        out_shape=(jax.ShapeDtypeStruct((n_rows * ns, LANES), U32),
                   jax.ShapeDtypeStruct((TOP_K, tokens), I32)),
        grid_spec=pltpu.PrefetchScalarGridSpec(
            num_scalar_prefetch=2,
            grid=(tokens // td,),
            in_specs=[smem_blk, smem_blk, pl.BlockSpec((td * ns, LANES), lambda i, *_: (i, 0))],
            out_specs=(pl.BlockSpec(memory_space=pl.ANY), smem_blk),
            scratch_shapes=[pltpu.VMEM((ns, LANES), U32),
                            pltpu.SemaphoreType.DMA, pltpu.SemaphoreType.DMA]),
        compiler_params=_cparams(("arbitrary",)),
        name="dispatch",
    )(pstart, counts, idx_t, rank_t, h2)


def _experts_kernel(bexp_ref, nused_ref, x_ref, w1_ref, w3_ref, w2_ref, y_ref, wb1, wb3, wb2):
    b = pl.program_id(0)
    live = b < nused_ref[0]
    changed = jnp.logical_or(b == 0, bexp_ref[b] != bexp_ref[jnp.maximum(b - 1, 0)])

    @pl.when(jnp.logical_and(live, changed))
    def _():
        wb1[...] = w1_ref[0, 0].astype(BF16)
        wb3[...] = w3_ref[0, 0].astype(BF16)
        wb2[...] = w2_ref[0, 0].astype(BF16)

    @pl.when(live)
    def _():
        ns = wb1.shape[0] // PAIR
        xk = _load_chunk_bf16(x_ref, 0, EXPERT_BLOCK, ns)
        h1 = jnp.dot(xk, wb1[0:PAIR, :], preferred_element_type=F32)
        h3 = jnp.dot(xk, wb3[0:PAIR, :], preferred_element_type=F32)
        for s in range(1, ns):
            xk = _load_chunk_bf16(x_ref, s, EXPERT_BLOCK, ns)
            h1 = h1 + jnp.dot(xk, wb1[s * PAIR:(s + 1) * PAIR, :], preferred_element_type=F32)
            h3 = h3 + jnp.dot(xk, wb3[s * PAIR:(s + 1) * PAIR, :], preferred_element_type=F32)
        act = (h1 * jax.nn.sigmoid(h1) * h3).astype(BF16)
        _store_rows(y_ref, jnp.dot(act, wb2[...], preferred_element_type=F32), EXPERT_BLOCK)


def _experts(block_exp, nused, xs, w1, w3, w2, layer):
    d, de = w1.shape[2], w1.shape[3]
    ns = d // PAIR
    nb = xs.shape[0] // (EXPERT_BLOCK * ns)
    blk = lambda b, be, nu: (jnp.minimum(b, nu[0] - 1), 0)
    wsel = lambda b, be, nu: (layer, be[jnp.minimum(b, nu[0] - 1)], 0, 0)
    return pl.pallas_call(
        _experts_kernel,
        out_shape=jax.ShapeDtypeStruct(xs.shape, U32),
        grid_spec=pltpu.PrefetchScalarGridSpec(
            num_scalar_prefetch=2,
            grid=(nb,),
            in_specs=[pl.BlockSpec((EXPERT_BLOCK * ns, LANES), blk),
                      pl.BlockSpec((1, 1, d, de), wsel),
                      pl.BlockSpec((1, 1, d, de), wsel),
                      pl.BlockSpec((1, 1, de, d), wsel)],
            out_specs=pl.BlockSpec((EXPERT_BLOCK * ns, LANES), blk),
            scratch_shapes=[pltpu.VMEM((d, de), BF16), pltpu.VMEM((d, de), BF16),
                            pltpu.VMEM((de, d), BF16)]),
        compiler_params=_cparams(("arbitrary",)),
        name="experts",
    )(block_exp, nused, xs, w1, w3, w2)


def _combine_kernel(pos_ref, y_ref, wts_ref, h_ref, x1_ref, g2_ref, w1_ref, w3_ref, w2_ref,
                    o_ref, buf, sem, *, tc, ns):
    def per_token(t, c):
        for k in range(TOP_K):
            pltpu.make_async_copy(_row_tile(y_ref, pos_ref[k, t], ns),
                                  _row_tile(buf.at[k], t, ns), sem).start(priority=k % 2)
        return c

    lax.fori_loop(0, tc, per_token, 0, unroll=2)

    hk = _load_chunk_bf16(h_ref, 0, tc, ns)
    h1 = jnp.dot(hk, w1_ref[0:PAIR, :], preferred_element_type=F32)
    h3 = jnp.dot(hk, w3_ref[0:PAIR, :], preferred_element_type=F32)
    for s in range(1, ns):
        hk = _load_chunk_bf16(h_ref, s, tc, ns)
        h1 = h1 + jnp.dot(hk, w1_ref[s * PAIR:(s + 1) * PAIR, :], preferred_element_type=F32)
        h3 = h3 + jnp.dot(hk, w3_ref[s * PAIR:(s + 1) * PAIR, :], preferred_element_type=F32)
    act = (h1 * jax.nn.sigmoid(h1) * h3).astype(BF16)
    shared = jnp.dot(act, w2_ref[...], preferred_element_type=F32)

    for k in range(TOP_K):
        pltpu.make_async_copy(y_ref.at[pl.ds(0, tc * ns)], buf.at[k], sem).wait()

    w = wts_ref[...]
    g2 = g2_ref[0]
    for s in range(ns):
        r_lo, r_hi = _unpack_pair(buf.at[0][pl.ds(s, tc, stride=ns), :])
        r_lo = r_lo * w[:, 0:1]
        r_hi = r_hi * w[:, 0:1]
        for k in range(1, TOP_K):
            y_lo, y_hi = _unpack_pair(buf.at[k][pl.ds(s, tc, stride=ns), :])
            r_lo = r_lo + y_lo * w[:, k:k + 1]
            r_hi = r_hi + y_hi * w[:, k:k + 1]
        c0 = slice(s * PAIR, s * PAIR + LANES)
        c1 = slice(s * PAIR + LANES, (s + 1) * PAIR)
        o_ref[0, :, c0] = x1_ref[0, :, c0] + g2[:, c0] * (r_lo + shared[:, c0])
        o_ref[0, :, c1] = x1_ref[0, :, c1] + g2[:, c1] * (r_hi + shared[:, c1])


def _combine(pos_t, y, wts_tok, h2, x1, g2, w1s, w3s, w2s):
    bsz, seq, d = x1.shape
    ns = d // PAIR
    de = w1s.shape[1]
    tc = min(256, seq)
    nt = seq // tc
    kern = functools.partial(_combine_kernel, tc=tc, ns=ns)
    row = lambda b, i: (b, i, 0)
    const2 = lambda b, i: (0, 0)
    return pl.pallas_call(
        kern,
        out_shape=jax.ShapeDtypeStruct((bsz, seq, d), F32),
        grid=(bsz, nt),
        in_specs=[pl.BlockSpec((TOP_K, tc), lambda b, i: (0, b * nt + i),
                               memory_space=pltpu.SMEM),
                  pl.BlockSpec(memory_space=pl.ANY),
                  pl.BlockSpec((tc, TOP_K), lambda b, i: (b * nt + i, 0)),
                  pl.BlockSpec((tc * ns, LANES), lambda b, i: (b * nt + i, 0)),
                  pl.BlockSpec((1, tc, d), row),
                  pl.BlockSpec((1, 1, d), lambda b, i: (b, 0, 0)),
                  _resident((d, de), const2),
                  _resident((d, de), const2),
                  _resident((de, d), const2)],
        out_specs=pl.BlockSpec((1, tc, d), row),
        scratch_shapes=[pltpu.VMEM((TOP_K, tc * ns, LANES), U32), pltpu.SemaphoreType.DMA],
        compiler_params=_cparams(("arbitrary", "arbitrary")),
        name="combine",
    )(pos_t, y, wts_tok, h2, x1, g2, w1s, w3s, w2s)


def _final_kernel(x_ref, g_ref, o_ref):
    x = x_ref[0]
    ms = jnp.mean(x * x, axis=-1, keepdims=True)
    o_ref[0] = (x * lax.rsqrt(ms + EPS)) * g_ref[...]


def _final_norm(x, g):
    bsz, seq, d = x.shape
    tm = min(512, seq)
    row = lambda b, i: (b, i, 0)
    return pl.pallas_call(
        _final_kernel,
        out_shape=jax.ShapeDtypeStruct(x.shape, F32),
        grid=(bsz, seq // tm),
        in_specs=[pl.BlockSpec((1, tm, d), row), pl.BlockSpec((1, d), lambda b, i: (0, 0))],
        out_specs=pl.BlockSpec((1, tm, d), row),
        compiler_params=_cparams(("arbitrary", "arbitrary")),
        name="final_norm",
    )(x, g)


def _pad_lanes(w, n):
    return jnp.pad(w, ((0, 0), (0, n - w.shape[1])))


def kernel(x, c, norm1_g, norm2_g, w_ada, b_ada, w_in, b_f, w_pool, pool_scale, w_o,
           w_router, router_bias, w1, w3, w2, w1_shared, w3_shared, w2_shared, final_g):
    bsz, seq, d = x.shape
    depth = w_ada.shape[0]
    n_heads = b_f.shape[1]
    da = n_heads * HEAD_DIM
    dp = d - da
    n_exp = w_router.shape[2]
    tokens = bsz * seq
    n_assign = tokens * TOP_K
    n_blocks = -(-n_assign // EXPERT_BLOCK) + n_exp
    n_rows = n_blocks * EXPERT_BLOCK

    mod = _ada(c, w_ada, b_ada)
    mod = mod.reshape(depth, bsz, N_MOD, 1, d)

    for l in range(depth):
        sh1, sc1, g1, sh2, sc2, g2 = (mod[l, :, j] for j in range(N_MOD))
        wl = w_in[l]
        wq = wl[:, :da].astype(BF16)
        wk = wl[:, da:2 * da].astype(BF16)
        wv = wl[:, 2 * da:3 * da].astype(BF16)
        wf = _pad_lanes(wl[:, 3 * da:3 * da + n_heads], LANES).astype(BF16)
        wu = wl[:, 3 * da + n_heads:].astype(BF16)
        bf = _pad_lanes(b_f[l][None, :], LANES)
        q, k, v, dcol, drow, p = _proj(
            x, norm1_g[l][None, :], sh1, sc1, wq, wk, wv, wf, bf, wu,
            w_pool[l].astype(BF16), pool_scale[l][None, :], n_heads=n_heads)
        a = _attn(q, k, v, dcol, drow, n_heads=n_heads)

        wr = _pad_lanes(w_router[l], LANES)
        wr_hi = wr.astype(BF16)
        wr_lo = (wr - wr_hi.astype(F32)).astype(BF16)
        wo = w_o[l].astype(BF16)
        x1, h2, idx_t, wts_t, rank_t, counts = _post(
            a, p, x, g1, norm2_g[l][None, :], sh2, sc2, wo[:da], wo[da:],
            jnp.concatenate([wr_hi, wr_lo], axis=1), wr_hi, router_bias[l][:, None])

        counts = counts[:, 0]
        padded = ((counts + EXPERT_BLOCK - 1) // EXPERT_BLOCK) * EXPERT_BLOCK
        cum = jnp.cumsum(padded)
        pstart = jnp.concatenate([jnp.zeros((1,), I32), cum]).astype(I32)
        nused = (cum[-1:] // EXPERT_BLOCK).astype(I32)
        block_starts = jnp.arange(n_blocks, dtype=I32) * EXPERT_BLOCK
        block_exp = jnp.minimum(
            jnp.sum((cum[None, :] <= block_starts[:, None]).astype(I32), axis=1), n_exp - 1)

        xs, pos_t = _dispatch(pstart, counts, idx_t, rank_t, h2, n_rows, d // PAIR)
        y = _experts(block_exp, nused, xs, w1, w3, w2, l)
        x = _combine(pos_t, y, wts_t.T, h2, x1, g2, w1_shared[l].astype(BF16),
                     w3_shared[l].astype(BF16), w2_shared[l].astype(BF16))
    return _final_norm(x, final_g[None, :])
```

```python
import functools

import jax
import jax.numpy as jnp
from jax import lax
from jax.experimental import pallas as pl
from jax.experimental.pallas import tpu as pltpu

F32 = jnp.float32
BF16 = jnp.bfloat16
I32 = jnp.int32
U32 = jnp.uint32

EPS = 1e-6
HEAD_DIM = 128
LANES = 128
POOL_WINDOWS = (2, 4, 8, 16)
POOL_HALO = 16
TOP_K = 8
N_EXPERT_GROUPS = 8
TOPK_GROUPS = 4
ROUTED_SCALE = 2.5
EXPERT_BLOCK = 512
N_MOD = 6
VMEM_LIMIT = 56 * 1024 * 1024

_NT = (((1,), (1,)), ((), ()))


def _cparams(sem):
    return pltpu.CompilerParams(dimension_semantics=sem, vmem_limit_bytes=VMEM_LIMIT)


PAIR = 2 * LANES


def _pack_pair(lo, hi):
    lo = lax.bitcast_convert_type(lo.astype(BF16).astype(F32), U32)
    hi = lax.bitcast_convert_type(hi.astype(BF16).astype(F32), U32)
    return hi | (lo >> 16)


def _unpack_pair(p):
    lo = lax.bitcast_convert_type(p << 16, F32)
    hi = lax.bitcast_convert_type(p & jnp.uint32(0xFFFF0000), F32)
    return lo, hi


def _store_rows(ref, x, m):
    ns = x.shape[1] // PAIR
    for s in range(ns):
        ref[pl.ds(s, m, stride=ns), :] = _pack_pair(x[:, s * PAIR:s * PAIR + LANES],
                                                    x[:, s * PAIR + LANES:(s + 1) * PAIR])


def _load_chunk_bf16(ref, s, m, ns):
    lo, hi = _unpack_pair(ref[pl.ds(s, m, stride=ns), :])
    return jnp.concatenate([lo.astype(BF16), hi.astype(BF16)], axis=1)


def _resident(shape, index_map):
    return pl.BlockSpec(shape, index_map, pipeline_mode=pl.Buffered(1))


def _ada_kernel(c_ref, w_ref, b_ref, o_ref):
    c = c_ref[...]
    ca = c * jax.nn.sigmoid(c)
    o_ref[0] = jnp.dot(ca, w_ref[0], precision=lax.Precision.HIGHEST,
                       preferred_element_type=F32) + b_ref[0]


def _ada(c, w_ada, b_ada):
    depth, d, n = w_ada.shape
    bsz = c.shape[0]
    rows = 8
    cp = jnp.zeros((rows, d), F32).at[:bsz].set(c)
    tn = 512
    out = pl.pallas_call(
        _ada_kernel,
        out_shape=jax.ShapeDtypeStruct((depth, rows, n), F32),
        grid=(depth, n // tn),
        in_specs=[pl.BlockSpec((rows, d), lambda l, j: (0, 0)),
                  pl.BlockSpec((1, d, tn), lambda l, j: (l, 0, j)),
                  pl.BlockSpec((1, 1, tn), lambda l, j: (l, 0, j))],
        out_specs=pl.BlockSpec((1, rows, tn), lambda l, j: (l, 0, j)),
        compiler_params=_cparams(("arbitrary", "arbitrary")),
        name="ada",
    )(cp, w_ada, b_ada.reshape(depth, 1, n))
    return out[:, :bsz]


def _proj_kernel(x_ref, g_ref, sh_ref, sc_ref, wq_ref, wk_ref, wv_ref, wf_ref, bf_ref,
                 wu_ref, wp_ref, ps_ref,
                 q_ref, k_ref, v_ref, dcol_ref, drow_ref, p_ref,
                 ubuf, dcarry, *, tm, n_heads, gc):
    i = pl.program_id(1)
    x = x_ref[0]
    ms = jnp.mean(x * x, axis=-1, keepdims=True)
    y = (x * lax.rsqrt(ms + EPS)) * g_ref[...]
    h = y * (1.0 + sc_ref[0]) + sh_ref[0]
    hb = h.astype(BF16)

    scale = HEAD_DIM ** -0.5
    q_ref[0] = (jnp.dot(hb, wq_ref[...], preferred_element_type=F32) * scale).astype(BF16)
    k_ref[0] = jnp.dot(hb, wk_ref[...], preferred_element_type=F32).astype(BF16)
    v_ref[0] = jnp.dot(hb, wv_ref[...], preferred_element_type=F32).astype(BF16)

    z = jnp.dot(hb, wf_ref[...], preferred_element_type=F32) + bf_ref[...]
    lf = jnp.minimum(z, 0.0) - jnp.log1p(jnp.exp(-jnp.abs(z)))
    r = lax.broadcasted_iota(I32, (tm, tm), 0)
    cidx = lax.broadcasted_iota(I32, (tm, tm), 1)
    tri = (cidx <= r).astype(F32)

    @pl.when(i == 0)
    def _():
        dcarry[...] = jnp.zeros_like(dcarry)
        ubuf[0:POOL_HALO, :] = jnp.zeros((POOL_HALO, ubuf.shape[1]), F32)

    @pl.when(i > 0)
    def _():
        ubuf[0:POOL_HALO, :] = ubuf[tm:tm + POOL_HALO, :]

    dc = jnp.dot(tri, lf, precision=lax.Precision.HIGHEST,
                 preferred_element_type=F32) + dcarry[...]
    dcarry[...] = dc[tm - 1:tm, :]
    dcol_ref[0] = dc
    drow_ref[0] = dc.T[0:drow_ref.shape[1], :]

    u = jnp.dot(hb, wu_ref[...], preferred_element_type=F32)
    ubuf[POOL_HALO:POOL_HALO + tm, :] = u
    pos = i * tm + lax.broadcasted_iota(I32, (tm, 1), 0)
    for g, w in enumerate(POOL_WINDOWS):
        s = ubuf[:, g * gc:(g + 1) * gc]
        step = 1
        while step < w:
            s = s + pltpu.roll(s, step, 0)
            step *= 2
        cnt = jnp.minimum(pos + 1, w).astype(F32)
        pooled = s[POOL_HALO:POOL_HALO + tm, :] / cnt - u[:, g * gc:(g + 1) * gc]
        yg = jnp.dot(pooled.astype(BF16), wp_ref[g], preferred_element_type=F32)
        p_ref[0, :, g * gc:(g + 1) * gc] = (yg * ps_ref[:, g * gc:(g + 1) * gc]).astype(BF16)


def _proj(x, g, sh, sc, wq, wk, wv, wf, bf, wu, wp, ps, *, n_heads):
    bsz, seq, d = x.shape
    da = wq.shape[1]
    dp = wu.shape[1]
    gc = dp // len(POOL_WINDOWS)
    tm = min(512, seq)
    nt = seq // tm
    kern = functools.partial(_proj_kernel, tm=tm, n_heads=n_heads, gc=gc)
    row = lambda b, i: (b, i, 0)
    per_b = lambda b, i: (b, 0, 0)
    const2 = lambda b, i: (0, 0)
    const3 = lambda b, i: (0, 0, 0)
    return pl.pallas_call(
        kern,
        out_shape=(jax.ShapeDtypeStruct((bsz, seq, da), BF16),
                   jax.ShapeDtypeStruct((bsz, seq, da), BF16),
                   jax.ShapeDtypeStruct((bsz, seq, da), BF16),
                   jax.ShapeDtypeStruct((bsz, seq, LANES), F32),
                   jax.ShapeDtypeStruct((bsz, 8, seq), F32),
                   jax.ShapeDtypeStruct((bsz, seq, dp), BF16)),
        grid=(bsz, nt),
        in_specs=[pl.BlockSpec((1, tm, d), row),
                  pl.BlockSpec((1, d), const2),
                  pl.BlockSpec((1, 1, d), per_b),
                  pl.BlockSpec((1, 1, d), per_b),
                  _resident((d, da), const2),
                  _resident((d, da), const2),
                  _resident((d, da), const2),
                  _resident((d, LANES), const2),
                  pl.BlockSpec((1, LANES), const2),
                  _resident((d, dp), const2),
                  _resident(wp.shape, const3),
                  pl.BlockSpec((1, dp), const2)],
        out_specs=(pl.BlockSpec((1, tm, da), row),
                   pl.BlockSpec((1, tm, da), row),
                   pl.BlockSpec((1, tm, da), row),
                   pl.BlockSpec((1, tm, LANES), row),
                   pl.BlockSpec((1, 8, tm), lambda b, i: (b, 0, i)),
                   pl.BlockSpec((1, tm, dp), row)),
        scratch_shapes=[pltpu.VMEM((tm + POOL_HALO, dp), F32),
                        pltpu.VMEM((1, LANES), F32)],
        compiler_params=_cparams(("arbitrary", "arbitrary")),
        name="proj",
    )(x, g, sh, sc, wq, wk, wv, wf, bf, wu, wp, ps)


def _attn_kernel(q_ref, k_ref, v_ref, dcol_ref, drow_ref, o_ref, *, tq, hp):
    hg = pl.program_id(1)
    i = pl.program_id(2)
    lane = lax.broadcasted_iota(I32, (tq, LANES), 1)
    dcol = dcol_ref[0]
    heads = [slice(u * HEAD_DIM, (u + 1) * HEAD_DIM) for u in range(hp)]
    qs = [q_ref[0, :, hs] for hs in heads]
    dqs = [jnp.sum(jnp.where(lane == hg * hp + u, dcol, 0.0), axis=-1, keepdims=True)
           for u in range(hp)]

    def block(j, carry, masked):
        start = pl.multiple_of(j * tq, tq)
        out = []
        for u in range(hp):
            m, l, acc = carry[u]
            ks = k_ref[0, pl.ds(start, tq), heads[u]]
            vs = v_ref[0, pl.ds(start, tq), heads[u]]
            dk = drow_ref[0, pl.ds(hg * hp + u, 1), pl.ds(start, tq)]
            z = lax.dot_general(qs[u], ks, _NT, preferred_element_type=F32) - dk
            if masked:
                rr = lax.broadcasted_iota(I32, (tq, tq), 0)
                cc = lax.broadcasted_iota(I32, (tq, tq), 1)
                z = jnp.where(cc <= rr, z, -jnp.inf)
            m_new = jnp.maximum(m, jnp.max(z, axis=-1, keepdims=True) + dqs[u])
            p = jnp.exp(z - (m_new - dqs[u]))
            alpha = jnp.exp(m - m_new)
            l = alpha * l + jnp.sum(p, axis=-1, keepdims=True)
            acc = alpha * acc + jnp.dot(p.astype(BF16), vs, preferred_element_type=F32)
            out.append((m_new, l, acc))
        return tuple(out)

    init = tuple((jnp.full((tq, 1), -jnp.inf, F32), jnp.zeros((tq, 1), F32),
                  jnp.zeros((tq, HEAD_DIM), F32)) for _ in range(hp))
    carry = lax.fori_loop(0, i, functools.partial(block, masked=False), init)
    carry = block(i, carry, True)
    for u in range(hp):
        _, l, acc = carry[u]
        o_ref[0, :, heads[u]] = (acc / l).astype(o_ref.dtype)


def _attn(q, k, v, dcol, drow, *, n_heads):
    bsz, seq, da = q.shape
    tq = min(512, seq)
    nq = seq // tq
    hp = 2 if n_heads % 2 == 0 else 1
    kern = functools.partial(_attn_kernel, tq=tq, hp=hp)
    return pl.pallas_call(
        kern,
        out_shape=jax.ShapeDtypeStruct((bsz, seq, da), BF16),
        grid=(bsz, n_heads // hp, nq),
        in_specs=[pl.BlockSpec((1, tq, hp * HEAD_DIM), lambda b, h, i: (b, i, h)),
                  pl.BlockSpec((1, seq, hp * HEAD_DIM), lambda b, h, i: (b, 0, h)),
                  pl.BlockSpec((1, seq, hp * HEAD_DIM), lambda b, h, i: (b, 0, h)),
                  pl.BlockSpec((1, tq, LANES), lambda b, h, i: (b, i, 0)),
                  pl.BlockSpec((1, 8, seq), lambda b, h, i: (b, 0, 0))],
        out_specs=pl.BlockSpec((1, tq, hp * HEAD_DIM), lambda b, h, i: (b, i, h)),
        compiler_params=_cparams(("arbitrary", "arbitrary", "arbitrary")),
        name="attn",
    )(q, k, v, dcol, drow)


def _post_kernel(a_ref, p_ref, x_ref, g1_ref, ng_ref, sh_ref, sc_ref, woa_ref, wop_ref,
                 wr1_ref, wr2_ref, rb_ref,
                 x1_ref, h2_ref, idx_ref, wts_ref, rank_ref, cnt_ref,
                 carry, *, tm, n_exp):
    first = jnp.logical_and(pl.program_id(0) == 0, pl.program_id(1) == 0)

    @pl.when(first)
    def _():
        carry[...] = jnp.zeros_like(carry)

    mix = jnp.dot(a_ref[0], woa_ref[...], preferred_element_type=F32)
    mix = mix + jnp.dot(p_ref[0], wop_ref[...], preferred_element_type=F32)
    x1 = x_ref[0] + g1_ref[0] * mix
    x1_ref[0] = x1
    ms = jnp.mean(x1 * x1, axis=-1, keepdims=True)
    y = (x1 * lax.rsqrt(ms + EPS)) * ng_ref[...]
    h2 = y * (1.0 + sc_ref[0]) + sh_ref[0]
    _store_rows(h2_ref, h2, tm)

    h_hi = h2.astype(BF16)
    h_lo = (h2 - h_hi.astype(F32)).astype(BF16)
    a1 = jnp.dot(h_hi, wr1_ref[...], preferred_element_type=F32)
    a2 = jnp.dot(h_lo, wr2_ref[...], preferred_element_type=F32)
    logits = a1[:, :LANES] + a1[:, LANES:] + a2
    lt = logits.T[0:n_exp, :]
    sc = jax.nn.sigmoid(lt)
    biased = sc + rb_ref[...]

    epg = n_exp // N_EXPERT_GROUPS
    sub = lax.broadcasted_iota(I32, (epg, tm), 0)
    blocks, gscore = [], []
    for g in range(N_EXPERT_GROUPS):
        blk = biased[g * epg:(g + 1) * epg, :]
        m1 = jnp.max(blk, axis=0, keepdims=True)
        f1 = jnp.min(jnp.where(blk == m1, sub, epg), axis=0, keepdims=True)
        m2 = jnp.max(jnp.where(sub == f1, -jnp.inf, blk), axis=0, keepdims=True)
        blocks.append(blk)
        gscore.append(m1 + m2)
    masked = []
    for g in range(N_EXPERT_GROUPS):
        beats = jnp.zeros((1, tm), I32)
        for g2 in range(N_EXPERT_GROUPS):
            if g2 == g:
                continue
            win = (gscore[g2] > gscore[g]) if g2 > g else (gscore[g2] >= gscore[g])
            beats = beats + win.astype(I32)
        masked.append(jnp.where(beats < TOPK_GROUPS, blocks[g], -jnp.inf))
    v = jnp.concatenate(masked, axis=0)

    eio = lax.broadcasted_iota(I32, (n_exp, tm), 0)
    base = carry[...]
    hits, svals = [], []
    sel = jnp.zeros((n_exp, tm), F32)
    for k in range(TOP_K):
        m = jnp.max(v, axis=0, keepdims=True)
        idx = jnp.min(jnp.where(v == m, eio, n_exp), axis=0, keepdims=True)
        hit = eio == idx
        svals.append(jnp.sum(jnp.where(hit, sc, 0.0), axis=0, keepdims=True))
        v = jnp.where(hit, -jnp.inf, v)
        sel = sel + hit.astype(F32)
        hits.append(hit)
        idx_ref[k:k + 1, :] = idx
    total = svals[0]
    for k in range(1, TOP_K):
        total = total + svals[k]
    for k in range(TOP_K):
        wts_ref[k:k + 1, :] = svals[k] / total * ROUTED_SCALE

    rr = lax.broadcasted_iota(I32, (tm, tm), 0)
    cc = lax.broadcasted_iota(I32, (tm, tm), 1)
    upper = (rr < cc).astype(BF16)
    before = jnp.dot(sel.astype(BF16), upper, preferred_element_type=F32).astype(I32) + base
    for k in range(TOP_K):
        rank_ref[k:k + 1, :] = jnp.sum(jnp.where(hits[k], before, 0), axis=0, keepdims=True)
    new = base + jnp.sum(sel, axis=1, keepdims=True).astype(I32)
    carry[...] = new
    cnt_ref[...] = new


def _post(a, p, x, g1, ng, sh, sc, woa, wop, wr1, wr2, rb):
    bsz, seq, d = x.shape
    da = a.shape[2]
    dp = p.shape[2]
    n_exp = rb.shape[0]
    tm = min(512, seq)
    nt = seq // tm
    tokens = bsz * seq
    kern = functools.partial(_post_kernel, tm=tm, n_exp=n_exp)
    row = lambda b, i: (b, i, 0)
    per_b = lambda b, i: (b, 0, 0)
    const2 = lambda b, i: (0, 0)
    tok = lambda b, i: (0, b * nt + i)
    return pl.pallas_call(
        kern,
        out_shape=(jax.ShapeDtypeStruct((bsz, seq, d), F32),
                   jax.ShapeDtypeStruct((tokens * (d // PAIR), LANES), U32),
                   jax.ShapeDtypeStruct((TOP_K, tokens), I32),
                   jax.ShapeDtypeStruct((TOP_K, tokens), F32),
                   jax.ShapeDtypeStruct((TOP_K, tokens), I32),
                   jax.ShapeDtypeStruct((n_exp, 1), I32)),
        grid=(bsz, nt),
        in_specs=[pl.BlockSpec((1, tm, da), row),
                  pl.BlockSpec((1, tm, dp), row),
                  pl.BlockSpec((1, tm, d), row),
                  pl.BlockSpec((1, 1, d), per_b),
                  pl.BlockSpec((1, d), const2),
                  pl.BlockSpec((1, 1, d), per_b),
                  pl.BlockSpec((1, 1, d), per_b),
                  _resident((da, d), const2),
                  _resident((dp, d), const2),
                  _resident((d, 2 * LANES), const2),
                  _resident((d, LANES), const2),
                  pl.BlockSpec((n_exp, 1), const2)],
        out_specs=(pl.BlockSpec((1, tm, d), row),
                   pl.BlockSpec((tm * (d // PAIR), LANES), lambda b, i: (b * nt + i, 0)),
                   pl.BlockSpec((TOP_K, tm), tok),
                   pl.BlockSpec((TOP_K, tm), tok),
                   pl.BlockSpec((TOP_K, tm), tok),
                   pl.BlockSpec((n_exp, 1), const2)),
        scratch_shapes=[pltpu.VMEM((n_exp, 1), I32)],
        compiler_params=_cparams(("arbitrary", "arbitrary")),
        name="post",
    )(a, p, x, g1, ng, sh, sc, woa, wop, wr1, wr2, rb)


def _row_tile(ref, t, ns):
    start = t * ns if isinstance(t, int) else pl.multiple_of(t * ns, ns)
    return ref.at[pl.ds(start, ns)]


def _row_copy(src, s, dst, t, sem, ns):
    return pltpu.make_async_copy(_row_tile(src, s, ns), _row_tile(dst, t, ns), sem)


def _positions_kernel(pstart_ref, idx_ref, rank_ref, pos_ref, *, n_exp):
    idx = idx_ref[...]
    pos = rank_ref[...]
    for e in range(n_exp):
        pos = pos + jnp.where(idx == e, pstart_ref[e], 0)
    pos_ref[...] = pos


def _positions(pstart, idx_t, rank_t, n_exp):
    tokens = idx_t.shape[1]
    tp = min(4096, tokens)
    blk = pl.BlockSpec((TOP_K, tp), lambda i, *_: (0, i))
    return pl.pallas_call(
        functools.partial(_positions_kernel, n_exp=n_exp),
        out_shape=jax.ShapeDtypeStruct((TOP_K, tokens), I32),
        grid_spec=pltpu.PrefetchScalarGridSpec(
            num_scalar_prefetch=1, grid=(tokens // tp,), in_specs=[blk, blk], out_specs=blk),
        compiler_params=_cparams(("arbitrary",)),
        name="positions",
    )(pstart, idx_t, rank_t)


def _dispatch_kernel(pstart_ref, cnt_ref, pos_ref, h_ref, w1_ref, w3_ref, w2_ref,
                     xs_ref, sh_ref, zrow, sem, zsem, *, td, n_exp, ns):
    step = pl.program_id(0)

    @pl.when(step == 0)
    def _():
        zrow[...] = jnp.zeros_like(zrow)

        def per_expert(e, total):
            lo = pstart_ref[e] + cnt_ref[e]
            hi = pstart_ref[e + 1]

            def fill(r, c):
                _row_copy(zrow, 0, xs_ref, r, zsem, ns).start()
                return c

            lax.fori_loop(lo, hi, fill, 0)
            return total + (hi - lo)

        npad = lax.fori_loop(0, n_exp, per_expert, 0)

        def drain(r, c):
            _row_copy(zrow, 0, xs_ref, 0, zsem, ns).wait()
            return c

        lax.fori_loop(0, npad, drain, 0)

    for t in range(td):
        for k in range(TOP_K):
            _row_copy(h_ref, t, xs_ref, pos_ref[k, t], sem, ns).start(priority=k % 2)

    hk = _load_chunk_bf16(h_ref, 0, td, ns)
    h1 = jnp.dot(hk, w1_ref[0:PAIR, :], preferred_element_type=F32)
    h3 = jnp.dot(hk, w3_ref[0:PAIR, :], preferred_element_type=F32)
    for s in range(1, ns):
        hk = _load_chunk_bf16(h_ref, s, td, ns)
        h1 = h1 + jnp.dot(hk, w1_ref[s * PAIR:(s + 1) * PAIR, :], preferred_element_type=F32)
        h3 = h3 + jnp.dot(hk, w3_ref[s * PAIR:(s + 1) * PAIR, :], preferred_element_type=F32)
    act = (h1 * jax.nn.sigmoid(h1) * h3).astype(BF16)
    sh_ref[...] = jnp.dot(act, w2_ref[...], preferred_element_type=F32)

    for k in range(TOP_K):
        pltpu.make_async_copy(h_ref, xs_ref.at[pl.ds(0, td * ns)], sem).wait()


def _dispatch(pstart, counts, pos_t, h2, w1s, w3s, w2s, n_rows, ns):
    tokens = h2.shape[0] // ns
    n_exp = counts.shape[0]
    d, de = w1s.shape
    td = min(256, tokens)
    kern = functools.partial(_dispatch_kernel, td=td, n_exp=n_exp, ns=ns)
    const2 = lambda i, *_: (0, 0)
    return pl.pallas_call(
        kern,
        out_shape=(jax.ShapeDtypeStruct((n_rows * ns, LANES), U32),
                   jax.ShapeDtypeStruct((tokens, d), F32)),
        grid_spec=pltpu.PrefetchScalarGridSpec(
            num_scalar_prefetch=2,
            grid=(tokens // td,),
            in_specs=[pl.BlockSpec((TOP_K, td), lambda i, *_: (0, i), memory_space=pltpu.SMEM),
                      pl.BlockSpec((td * ns, LANES), lambda i, *_: (i, 0)),
                      _resident((d, de), const2),
                      _resident((d, de), const2),
                      _resident((de, d), const2)],
            out_specs=(pl.BlockSpec(memory_space=pl.ANY),
                       pl.BlockSpec((td, d), lambda i, *_: (i, 0))),
            scratch_shapes=[pltpu.VMEM((ns, LANES), U32),
                            pltpu.SemaphoreType.DMA, pltpu.SemaphoreType.DMA]),
        compiler_params=_cparams(("arbitrary",)),
        name="dispatch",
    )(pstart, counts, pos_t, h2, w1s, w3s, w2s)


def _experts_kernel(bexp_ref, nused_ref, x_ref, w1_ref, w3_ref, w2_ref, y_ref, wb1, wb3, wb2):
    b = pl.program_id(0)
    live = b < nused_ref[0]
    changed = jnp.logical_or(b == 0, bexp_ref[b] != bexp_ref[jnp.maximum(b - 1, 0)])

    @pl.when(jnp.logical_and(live, changed))
    def _():
        wb1[...] = w1_ref[0, 0].astype(BF16)
        wb3[...] = w3_ref[0, 0].astype(BF16)
        wb2[...] = w2_ref[0, 0].astype(BF16)

    @pl.when(live)
    def _():
        ns = wb1.shape[0] // PAIR
        xk = _load_chunk_bf16(x_ref, 0, EXPERT_BLOCK, ns)
        h1 = jnp.dot(xk, wb1[0:PAIR, :], preferred_element_type=F32)
        h3 = jnp.dot(xk, wb3[0:PAIR, :], preferred_element_type=F32)
        for s in range(1, ns):
            xk = _load_chunk_bf16(x_ref, s, EXPERT_BLOCK, ns)
            h1 = h1 + jnp.dot(xk, wb1[s * PAIR:(s + 1) * PAIR, :], preferred_element_type=F32)
            h3 = h3 + jnp.dot(xk, wb3[s * PAIR:(s + 1) * PAIR, :], preferred_element_type=F32)
        act = (h1 * jax.nn.sigmoid(h1) * h3).astype(BF16)
        _store_rows(y_ref, jnp.dot(act, wb2[...], preferred_element_type=F32), EXPERT_BLOCK)


def _experts(block_exp, nused, xs, w1, w3, w2, layer):
    d, de = w1.shape[2], w1.shape[3]
    ns = d // PAIR
    nb = xs.shape[0] // (EXPERT_BLOCK * ns)
    blk = lambda b, be, nu: (jnp.minimum(b, nu[0] - 1), 0)
    wsel = lambda b, be, nu: (layer, be[jnp.minimum(b, nu[0] - 1)], 0, 0)
    return pl.pallas_call(
        _experts_kernel,
        out_shape=jax.ShapeDtypeStruct(xs.shape, U32),
        grid_spec=pltpu.PrefetchScalarGridSpec(
            num_scalar_prefetch=2,
            grid=(nb,),
            in_specs=[pl.BlockSpec((EXPERT_BLOCK * ns, LANES), blk),
                      pl.BlockSpec((1, 1, d, de), wsel),
                      pl.BlockSpec((1, 1, d, de), wsel),
                      pl.BlockSpec((1, 1, de, d), wsel)],
            out_specs=pl.BlockSpec((EXPERT_BLOCK * ns, LANES), blk),
            scratch_shapes=[pltpu.VMEM((d, de), BF16), pltpu.VMEM((d, de), BF16),
                            pltpu.VMEM((de, d), BF16)]),
        compiler_params=_cparams(("arbitrary",)),
        name="experts",
    )(block_exp, nused, xs, w1, w3, w2)


def _combine_kernel(pos_ref, y_ref, wts_ref, sh_ref, x1_ref, g2_ref, o_ref, buf, sem, *, tc, ns):
    for t in range(tc):
        for k in range(TOP_K):
            pltpu.make_async_copy(_row_tile(y_ref, pos_ref[k, t], ns),
                                  _row_tile(buf.at[k], t, ns), sem).start(priority=k % 2)

    for k in range(TOP_K):
        pltpu.make_async_copy(y_ref.at[pl.ds(0, tc * ns)], buf.at[k], sem).wait()

    w = wts_ref[...]
    g2 = g2_ref[0]
    for s in range(ns):
        r_lo, r_hi = _unpack_pair(buf.at[0][pl.ds(s, tc, stride=ns), :])
        r_lo = r_lo * w[:, 0:1]
        r_hi = r_hi * w[:, 0:1]
        for k in range(1, TOP_K):
            y_lo, y_hi = _unpack_pair(buf.at[k][pl.ds(s, tc, stride=ns), :])
            r_lo = r_lo + y_lo * w[:, k:k + 1]
            r_hi = r_hi + y_hi * w[:, k:k + 1]
        c0 = slice(s * PAIR, s * PAIR + LANES)
        c1 = slice(s * PAIR + LANES, (s + 1) * PAIR)
        o_ref[0, :, c0] = x1_ref[0, :, c0] + g2[:, c0] * (r_lo + sh_ref[0, :, c0])
        o_ref[0, :, c1] = x1_ref[0, :, c1] + g2[:, c1] * (r_hi + sh_ref[0, :, c1])


def _combine(pos_t, y, wts_tok, shared, x1, g2):
    bsz, seq, d = x1.shape
    ns = d // PAIR
    tc = min(256, seq)
    nt = seq // tc
    kern = functools.partial(_combine_kernel, tc=tc, ns=ns)
    row = lambda b, i: (b, i, 0)
    return pl.pallas_call(
        kern,
        out_shape=jax.ShapeDtypeStruct((bsz, seq, d), F32),
        grid=(bsz, nt),
        in_specs=[pl.BlockSpec((TOP_K, tc), lambda b, i: (0, b * nt + i),
                               memory_space=pltpu.SMEM),
                  pl.BlockSpec(memory_space=pl.ANY),
                  pl.BlockSpec((tc, TOP_K), lambda b, i: (b * nt + i, 0)),
                  pl.BlockSpec((1, tc, d), row),
                  pl.BlockSpec((1, tc, d), row),
                  pl.BlockSpec((1, 1, d), lambda b, i: (b, 0, 0))],
        out_specs=pl.BlockSpec((1, tc, d), row),
        scratch_shapes=[pltpu.VMEM((TOP_K, tc * ns, LANES), U32), pltpu.SemaphoreType.DMA],
        compiler_params=_cparams(("arbitrary", "arbitrary")),
        name="combine",
    )(pos_t, y, wts_tok, shared, x1, g2)


def _final_kernel(x_ref, g_ref, o_ref):
    x = x_ref[0]
    ms = jnp.mean(x * x, axis=-1, keepdims=True)
    o_ref[0] = (x * lax.rsqrt(ms + EPS)) * g_ref[...]


def _final_norm(x, g):
    bsz, seq, d = x.shape
    tm = min(512, seq)
    row = lambda b, i: (b, i, 0)
    return pl.pallas_call(
        _final_kernel,
        out_shape=jax.ShapeDtypeStruct(x.shape, F32),
        grid=(bsz, seq // tm),
        in_specs=[pl.BlockSpec((1, tm, d), row), pl.BlockSpec((1, d), lambda b, i: (0, 0))],
        out_specs=pl.BlockSpec((1, tm, d), row),
        compiler_params=_cparams(("arbitrary", "arbitrary")),
        name="final_norm",
    )(x, g)


def _pad_lanes(w, n):
    return jnp.pad(w, ((0, 0), (0, n - w.shape[1])))


def kernel(x, c, norm1_g, norm2_g, w_ada, b_ada, w_in, b_f, w_pool, pool_scale, w_o,
           w_router, router_bias, w1, w3, w2, w1_shared, w3_shared, w2_shared, final_g):
    bsz, seq, d = x.shape
    depth = w_ada.shape[0]
    n_heads = b_f.shape[1]
    da = n_heads * HEAD_DIM
    dp = d - da
    n_exp = w_router.shape[2]
    tokens = bsz * seq
    n_assign = tokens * TOP_K
    n_blocks = -(-n_assign // EXPERT_BLOCK) + n_exp
    n_rows = n_blocks * EXPERT_BLOCK

    mod = _ada(c, w_ada, b_ada)
    mod = mod.reshape(depth, bsz, N_MOD, 1, d)

    for l in range(depth):
        sh1, sc1, g1, sh2, sc2, g2 = (mod[l, :, j] for j in range(N_MOD))
        wl = w_in[l]
        wq = wl[:, :da].astype(BF16)
        wk = wl[:, da:2 * da].astype(BF16)
        wv = wl[:, 2 * da:3 * da].astype(BF16)
        wf = _pad_lanes(wl[:, 3 * da:3 * da + n_heads], LANES).astype(BF16)
        wu = wl[:, 3 * da + n_heads:].astype(BF16)
        bf = _pad_lanes(b_f[l][None, :], LANES)
        q, k, v, dcol, drow, p = _proj(
            x, norm1_g[l][None, :], sh1, sc1, wq, wk, wv, wf, bf, wu,
            w_pool[l].astype(BF16), pool_scale[l][None, :], n_heads=n_heads)
        a = _attn(q, k, v, dcol, drow, n_heads=n_heads)

        wr = _pad_lanes(w_router[l], LANES)
        wr_hi = wr.astype(BF16)
        wr_lo = (wr - wr_hi.astype(F32)).astype(BF16)
        wo = w_o[l].astype(BF16)
        x1, h2, idx_t, wts_t, rank_t, counts = _post(
            a, p, x, g1, norm2_g[l][None, :], sh2, sc2, wo[:da], wo[da:],
            jnp.concatenate([wr_hi, wr_lo], axis=1), wr_hi, router_bias[l][:, None])

        counts = counts[:, 0]
        padded = ((counts + EXPERT_BLOCK - 1) // EXPERT_BLOCK) * EXPERT_BLOCK
        cum = jnp.cumsum(padded)
        pstart = jnp.concatenate([jnp.zeros((1,), I32), cum]).astype(I32)
        nused = (cum[-1:] // EXPERT_BLOCK).astype(I32)
        block_starts = jnp.arange(n_blocks, dtype=I32) * EXPERT_BLOCK
        block_exp = jnp.minimum(
            jnp.sum((cum[None, :] <= block_starts[:, None]).astype(I32), axis=1), n_exp - 1)

        pos_t = _positions(pstart, idx_t, rank_t, n_exp)
        xs, shared = _dispatch(pstart, counts, pos_t, h2, w1_shared[l].astype(BF16),
                               w3_shared[l].astype(BF16), w2_shared[l].astype(BF16),
                               n_rows, d // PAIR)
        y = _experts(block_exp, nused, xs, w1, w3, w2, l)
        x = _combine(pos_t, y, wts_t.T, shared.reshape(bsz, seq, d), x1, g2)
    return _final_norm(x, final_g[None, :])
```

```python
import functools

import jax
import jax.numpy as jnp
from jax import lax
from jax.experimental import pallas as pl
from jax.experimental.pallas import tpu as pltpu

F32 = jnp.float32
BF16 = jnp.bfloat16
I32 = jnp.int32
U32 = jnp.uint32

EPS = 1e-6
LOG2E = 1.4426950408889634
HEAD_DIM = 128
LANES = 128
POOL_WINDOWS = (2, 4, 8, 16)
POOL_HALO = 16
TOP_K = 8
N_EXPERT_GROUPS = 8
TOPK_GROUPS = 4
ROUTED_SCALE = 2.5
EXPERT_BLOCK = 512
N_MOD = 6
VMEM_LIMIT = 56 * 1024 * 1024

_NT = (((1,), (1,)), ((), ()))


def _cparams(sem):
    return pltpu.CompilerParams(dimension_semantics=sem, vmem_limit_bytes=VMEM_LIMIT)


PAIR = 2 * LANES


def _pack_pair(lo, hi):
    lo = lax.bitcast_convert_type(lo.astype(BF16).astype(F32), U32)
    hi = lax.bitcast_convert_type(hi.astype(BF16).astype(F32), U32)
    return hi | (lo >> 16)


def _unpack_pair(p):
    lo = lax.bitcast_convert_type(p << 16, F32)
    hi = lax.bitcast_convert_type(p & jnp.uint32(0xFFFF0000), F32)
    return lo, hi


def _store_rows(ref, x, m):
    ns = x.shape[1] // PAIR
    for s in range(ns):
        ref[pl.ds(s, m, stride=ns), :] = _pack_pair(x[:, s * PAIR:s * PAIR + LANES],
                                                    x[:, s * PAIR + LANES:(s + 1) * PAIR])


def _load_chunk_bf16(ref, s, m, ns):
    lo, hi = _unpack_pair(ref[pl.ds(s, m, stride=ns), :])
    return jnp.concatenate([lo.astype(BF16), hi.astype(BF16)], axis=1)


def _resident(shape, index_map):
    return pl.BlockSpec(shape, index_map, pipeline_mode=pl.Buffered(1))


def _ada_kernel(c_ref, w_ref, b_ref, o_ref):
    c = c_ref[...]
    ca = c * jax.nn.sigmoid(c)
    o_ref[0] = jnp.dot(ca, w_ref[0], precision=lax.Precision.HIGHEST,
                       preferred_element_type=F32) + b_ref[0]


def _ada(c, w_ada, b_ada):
    depth, d, n = w_ada.shape
    bsz = c.shape[0]
    rows = 8
    cp = jnp.zeros((rows, d), F32).at[:bsz].set(c)
    tn = 512
    out = pl.pallas_call(
        _ada_kernel,
        out_shape=jax.ShapeDtypeStruct((depth, rows, n), F32),
        grid=(depth, n // tn),
        in_specs=[pl.BlockSpec((rows, d), lambda l, j: (0, 0)),
                  pl.BlockSpec((1, d, tn), lambda l, j: (l, 0, j)),
                  pl.BlockSpec((1, 1, tn), lambda l, j: (l, 0, j))],
        out_specs=pl.BlockSpec((1, rows, tn), lambda l, j: (l, 0, j)),
        compiler_params=_cparams(("arbitrary", "arbitrary")),
        name="ada",
    )(cp, w_ada, b_ada.reshape(depth, 1, n))
    return out[:, :bsz]


def _proj_kernel(x_ref, g_ref, sh_ref, sc_ref, wq_ref, wk_ref, wv_ref, wf_ref, bf_ref,
                 wu_ref, wp_ref, ps_ref,
                 q_ref, k_ref, v_ref, dcol_ref, drow_ref, p_ref,
                 ubuf, dcarry, *, tm, n_heads, gc):
    i = pl.program_id(1)
    x = x_ref[0]
    ms = jnp.mean(x * x, axis=-1, keepdims=True)
    y = (x * lax.rsqrt(ms + EPS)) * g_ref[...]
    h = y * (1.0 + sc_ref[0]) + sh_ref[0]
    hb = h.astype(BF16)

    scale = HEAD_DIM ** -0.5 * LOG2E
    q_ref[0] = (jnp.dot(hb, wq_ref[...], preferred_element_type=F32) * scale).astype(BF16)
    k_ref[0] = jnp.dot(hb, wk_ref[...], preferred_element_type=F32).astype(BF16)
    v_ref[0] = jnp.dot(hb, wv_ref[...], preferred_element_type=F32).astype(BF16)

    z = jnp.dot(hb, wf_ref[...], preferred_element_type=F32) + bf_ref[...]
    lf = (jnp.minimum(z, 0.0) - jnp.log1p(jnp.exp(-jnp.abs(z)))) * LOG2E
    r = lax.broadcasted_iota(I32, (tm, tm), 0)
    cidx = lax.broadcasted_iota(I32, (tm, tm), 1)
    tri = (cidx <= r).astype(F32)

    @pl.when(i == 0)
    def _():
        dcarry[...] = jnp.zeros_like(dcarry)
        ubuf[0:POOL_HALO, :] = jnp.zeros((POOL_HALO, ubuf.shape[1]), F32)

    @pl.when(i > 0)
    def _():
        ubuf[0:POOL_HALO, :] = ubuf[tm:tm + POOL_HALO, :]

    dc = jnp.dot(tri, lf, precision=lax.Precision.HIGHEST,
                 preferred_element_type=F32) + dcarry[...]
    dcarry[...] = dc[tm - 1:tm, :]
    dcol_ref[0] = dc
    drow_ref[0] = dc.T[0:drow_ref.shape[1], :]

    u = jnp.dot(hb, wu_ref[...], preferred_element_type=F32)
    ubuf[POOL_HALO:POOL_HALO + tm, :] = u
    pos = i * tm + lax.broadcasted_iota(I32, (tm, 1), 0)
    for g, w in enumerate(POOL_WINDOWS):
        s = ubuf[:, g * gc:(g + 1) * gc]
        step = 1
        while step < w:
            s = s + pltpu.roll(s, step, 0)
            step *= 2
        cnt = jnp.minimum(pos + 1, w).astype(F32)
        pooled = s[POOL_HALO:POOL_HALO + tm, :] / cnt - u[:, g * gc:(g + 1) * gc]
        yg = jnp.dot(pooled.astype(BF16), wp_ref[g], preferred_element_type=F32)
        p_ref[0, :, g * gc:(g + 1) * gc] = (yg * ps_ref[:, g * gc:(g + 1) * gc]).astype(BF16)


def _proj(x, g, sh, sc, wq, wk, wv, wf, bf, wu, wp, ps, *, n_heads):
    bsz, seq, d = x.shape
    da = wq.shape[1]
    dp = wu.shape[1]
    gc = dp // len(POOL_WINDOWS)
    tm = min(512, seq)
    nt = seq // tm
    kern = functools.partial(_proj_kernel, tm=tm, n_heads=n_heads, gc=gc)
    row = lambda b, i: (b, i, 0)
    per_b = lambda b, i: (b, 0, 0)
    const2 = lambda b, i: (0, 0)
    const3 = lambda b, i: (0, 0, 0)
    return pl.pallas_call(
        kern,
        out_shape=(jax.ShapeDtypeStruct((bsz, seq, da), BF16),
                   jax.ShapeDtypeStruct((bsz, seq, da), BF16),
                   jax.ShapeDtypeStruct((bsz, seq, da), BF16),
                   jax.ShapeDtypeStruct((bsz, seq, LANES), F32),
                   jax.ShapeDtypeStruct((bsz, 8, seq), F32),
                   jax.ShapeDtypeStruct((bsz, seq, dp), BF16)),
        grid=(bsz, nt),
        in_specs=[pl.BlockSpec((1, tm, d), row),
                  pl.BlockSpec((1, d), const2),
                  pl.BlockSpec((1, 1, d), per_b),
                  pl.BlockSpec((1, 1, d), per_b),
                  _resident((d, da), const2),
                  _resident((d, da), const2),
                  _resident((d, da), const2),
                  _resident((d, LANES), const2),
                  pl.BlockSpec((1, LANES), const2),
                  _resident((d, dp), const2),
                  _resident(wp.shape, const3),
                  pl.BlockSpec((1, dp), const2)],
        out_specs=(pl.BlockSpec((1, tm, da), row),
                   pl.BlockSpec((1, tm, da), row),
                   pl.BlockSpec((1, tm, da), row),
                   pl.BlockSpec((1, tm, LANES), row),
                   pl.BlockSpec((1, 8, tm), lambda b, i: (b, 0, i)),
                   pl.BlockSpec((1, tm, dp), row)),
        scratch_shapes=[pltpu.VMEM((tm + POOL_HALO, dp), F32),
                        pltpu.VMEM((1, LANES), F32)],
        compiler_params=_cparams(("arbitrary", "arbitrary")),
        name="proj",
    )(x, g, sh, sc, wq, wk, wv, wf, bf, wu, wp, ps)


def _attn_kernel(q_ref, k_ref, v_ref, dcol_ref, drow_ref, o_ref, *, tq, hp):
    hg = pl.program_id(1)
    i = pl.program_id(2)
    lane = lax.broadcasted_iota(I32, (tq, LANES), 1)
    dcol = dcol_ref[0]
    heads = [slice(u * HEAD_DIM, (u + 1) * HEAD_DIM) for u in range(hp)]
    qs = [q_ref[0, :, hs] for hs in heads]
    dqs = [jnp.sum(jnp.where(lane == hg * hp + u, dcol, 0.0), axis=-1, keepdims=True)
           for u in range(hp)]

    def block(j, carry, masked):
        start = pl.multiple_of(j * tq, tq)
        out = []
        for u in range(hp):
            m, l, acc = carry[u]
            ks = k_ref[0, pl.ds(start, tq), heads[u]]
            vs = v_ref[0, pl.ds(start, tq), heads[u]]
            dk = drow_ref[0, pl.ds(hg * hp + u, 1), pl.ds(start, tq)]
            z = lax.dot_general(qs[u], ks, _NT, preferred_element_type=F32) - dk
            if masked:
                rr = lax.broadcasted_iota(I32, (tq, tq), 0)
                cc = lax.broadcasted_iota(I32, (tq, tq), 1)
                z = jnp.where(cc <= rr, z, -jnp.inf)
            m_new = jnp.maximum(m, jnp.max(z, axis=-1, keepdims=True) + dqs[u])
            p = jnp.exp2(z - (m_new - dqs[u]))
            alpha = jnp.exp2(m - m_new)
            l = alpha * l + jnp.sum(p, axis=-1, keepdims=True)
            acc = alpha * acc + jnp.dot(p.astype(BF16), vs, preferred_element_type=F32)
            out.append((m_new, l, acc))
        return tuple(out)

    init = tuple((jnp.full((tq, 1), -jnp.inf, F32), jnp.zeros((tq, 1), F32),
                  jnp.zeros((tq, HEAD_DIM), F32)) for _ in range(hp))
    carry = lax.fori_loop(0, i, functools.partial(block, masked=False), init)
    carry = block(i, carry, True)
    for u in range(hp):
        _, l, acc = carry[u]
        o_ref[0, :, heads[u]] = (acc / l).astype(o_ref.dtype)


def _attn(q, k, v, dcol, drow, *, n_heads):
    bsz, seq, da = q.shape
    tq = min(1024, seq)
    nq = seq // tq
    hp = 2 if n_heads % 2 == 0 else 1
    kern = functools.partial(_attn_kernel, tq=tq, hp=hp)
    return pl.pallas_call(
        kern,
        out_shape=jax.ShapeDtypeStruct((bsz, seq, da), BF16),
        grid=(bsz, n_heads // hp, nq),
        in_specs=[pl.BlockSpec((1, tq, hp * HEAD_DIM), lambda b, h, i: (b, i, h)),
                  pl.BlockSpec((1, seq, hp * HEAD_DIM), lambda b, h, i: (b, 0, h)),
                  pl.BlockSpec((1, seq, hp * HEAD_DIM), lambda b, h, i: (b, 0, h)),
                  pl.BlockSpec((1, tq, LANES), lambda b, h, i: (b, i, 0)),
                  pl.BlockSpec((1, 8, seq), lambda b, h, i: (b, 0, 0))],
        out_specs=pl.BlockSpec((1, tq, hp * HEAD_DIM), lambda b, h, i: (b, i, h)),
        compiler_params=_cparams(("arbitrary", "arbitrary", "arbitrary")),
        name="attn",
    )(q, k, v, dcol, drow)


def _post_kernel(a_ref, p_ref, x_ref, g1_ref, ng_ref, sh_ref, sc_ref, woa_ref, wop_ref,
                 wr1_ref, wr2_ref, rb_ref,
                 x1_ref, h2_ref, idx_ref, wts_ref, rank_ref, cnt_ref,
                 carry, *, tm, n_exp):
    first = jnp.logical_and(pl.program_id(0) == 0, pl.program_id(1) == 0)

    @pl.when(first)
    def _():
        carry[...] = jnp.zeros_like(carry)

    mix = jnp.dot(a_ref[0], woa_ref[...], preferred_element_type=F32)
    mix = mix + jnp.dot(p_ref[0], wop_ref[...], preferred_element_type=F32)
    x1 = x_ref[0] + g1_ref[0] * mix
    x1_ref[0] = x1
    ms = jnp.mean(x1 * x1, axis=-1, keepdims=True)
    y = (x1 * lax.rsqrt(ms + EPS)) * ng_ref[...]
    h2 = y * (1.0 + sc_ref[0]) + sh_ref[0]
    _store_rows(h2_ref, h2, tm)

    h_hi = h2.astype(BF16)
    h_lo = (h2 - h_hi.astype(F32)).astype(BF16)
    a1 = jnp.dot(h_hi, wr1_ref[...], preferred_element_type=F32)
    a2 = jnp.dot(h_lo, wr2_ref[...], preferred_element_type=F32)
    logits = a1[:, :LANES] + a1[:, LANES:] + a2
    lt = logits.T[0:n_exp, :]
    sc = jax.nn.sigmoid(lt)
    biased = sc + rb_ref[...]

    epg = n_exp // N_EXPERT_GROUPS
    sub = lax.broadcasted_iota(I32, (epg, tm), 0)
    blocks, gscore = [], []
    for g in range(N_EXPERT_GROUPS):
        blk = biased[g * epg:(g + 1) * epg, :]
        m1 = jnp.max(blk, axis=0, keepdims=True)
        f1 = jnp.min(jnp.where(blk == m1, sub, epg), axis=0, keepdims=True)
        m2 = jnp.max(jnp.where(sub == f1, -jnp.inf, blk), axis=0, keepdims=True)
        blocks.append(blk)
        gscore.append(m1 + m2)
    masked = []
    for g in range(N_EXPERT_GROUPS):
        beats = jnp.zeros((1, tm), I32)
        for g2 in range(N_EXPERT_GROUPS):
            if g2 == g:
                continue
            win = (gscore[g2] > gscore[g]) if g2 > g else (gscore[g2] >= gscore[g])
            beats = beats + win.astype(I32)
        masked.append(jnp.where(beats < TOPK_GROUPS, blocks[g], -jnp.inf))
    v = jnp.concatenate(masked, axis=0)

    eio = lax.broadcasted_iota(I32, (n_exp, tm), 0)
    base = carry[...]
    hits, svals = [], []
    sel = jnp.zeros((n_exp, tm), F32)
    for k in range(TOP_K):
        m = jnp.max(v, axis=0, keepdims=True)
        idx = jnp.min(jnp.where(v == m, eio, n_exp), axis=0, keepdims=True)
        hit = eio == idx
        svals.append(jnp.sum(jnp.where(hit, sc, 0.0), axis=0, keepdims=True))
        v = jnp.where(hit, -jnp.inf, v)
        sel = sel + hit.astype(F32)
        hits.append(hit)
        idx_ref[k:k + 1, :] = idx
    total = svals[0]
    for k in range(1, TOP_K):
        total = total + svals[k]
    for k in range(TOP_K):
        wts_ref[k:k + 1, :] = svals[k] / total * ROUTED_SCALE

    rr = lax.broadcasted_iota(I32, (tm, tm), 0)
    cc = lax.broadcasted_iota(I32, (tm, tm), 1)
    upper = (rr < cc).astype(BF16)
    before = jnp.dot(sel.astype(BF16), upper, preferred_element_type=F32).astype(I32) + base
    for k in range(TOP_K):
        rank_ref[k:k + 1, :] = jnp.sum(jnp.where(hits[k], before, 0), axis=0, keepdims=True)
    new = base + jnp.sum(sel, axis=1, keepdims=True).astype(I32)
    carry[...] = new
    cnt_ref[...] = new


def _post(a, p, x, g1, ng, sh, sc, woa, wop, wr1, wr2, rb):
    bsz, seq, d = x.shape
    da = a.shape[2]
    dp = p.shape[2]
    n_exp = rb.shape[0]
    tm = min(512, seq)
    nt = seq // tm
    tokens = bsz * seq
    kern = functools.partial(_post_kernel, tm=tm, n_exp=n_exp)
    row = lambda b, i: (b, i, 0)
    per_b = lambda b, i: (b, 0, 0)
    const2 = lambda b, i: (0, 0)
    tok = lambda b, i: (0, b * nt + i)
    return pl.pallas_call(
        kern,
        out_shape=(jax.ShapeDtypeStruct((bsz, seq, d), F32),
                   jax.ShapeDtypeStruct((tokens * (d // PAIR), LANES), U32),
                   jax.ShapeDtypeStruct((TOP_K, tokens), I32),
                   jax.ShapeDtypeStruct((TOP_K, tokens), F32),
                   jax.ShapeDtypeStruct((TOP_K, tokens), I32),
                   jax.ShapeDtypeStruct((n_exp, 1), I32)),
        grid=(bsz, nt),
        in_specs=[pl.BlockSpec((1, tm, da), row),
                  pl.BlockSpec((1, tm, dp), row),
                  pl.BlockSpec((1, tm, d), row),
                  pl.BlockSpec((1, 1, d), per_b),
                  pl.BlockSpec((1, d), const2),
                  pl.BlockSpec((1, 1, d), per_b),
                  pl.BlockSpec((1, 1, d), per_b),
                  _resident((da, d), const2),
                  _resident((dp, d), const2),
                  _resident((d, 2 * LANES), const2),
                  _resident((d, LANES), const2),
                  pl.BlockSpec((n_exp, 1), const2)],
        out_specs=(pl.BlockSpec((1, tm, d), row),
                   pl.BlockSpec((tm * (d // PAIR), LANES), lambda b, i: (b * nt + i, 0)),
                   pl.BlockSpec((TOP_K, tm), tok),
                   pl.BlockSpec((TOP_K, tm), tok),
                   pl.BlockSpec((TOP_K, tm), tok),
                   pl.BlockSpec((n_exp, 1), const2)),
        scratch_shapes=[pltpu.VMEM((n_exp, 1), I32)],
        compiler_params=_cparams(("arbitrary", "arbitrary")),
        name="post",
    )(a, p, x, g1, ng, sh, sc, woa, wop, wr1, wr2, rb)


def _row_tile(ref, t, ns):
    start = t * ns if isinstance(t, int) else pl.multiple_of(t * ns, ns)
    return ref.at[pl.ds(start, ns)]


def _row_copy(src, s, dst, t, sem, ns):
    return pltpu.make_async_copy(_row_tile(src, s, ns), _row_tile(dst, t, ns), sem)


def _positions_kernel(pstart_ref, idx_ref, rank_ref, pos_ref, *, n_exp):
    idx = idx_ref[...]
    pos = rank_ref[...]
    for e in range(n_exp):
        pos = pos + jnp.where(idx == e, pstart_ref[e], 0)
    pos_ref[...] = pos


def _positions(pstart, idx_t, rank_t, n_exp):
    tokens = idx_t.shape[1]
    tp = min(4096, tokens)
    blk = pl.BlockSpec((TOP_K, tp), lambda i, *_: (0, i))
    return pl.pallas_call(
        functools.partial(_positions_kernel, n_exp=n_exp),
        out_shape=jax.ShapeDtypeStruct((TOP_K, tokens), I32),
        grid_spec=pltpu.PrefetchScalarGridSpec(
            num_scalar_prefetch=1, grid=(tokens // tp,), in_specs=[blk, blk], out_specs=blk),
        compiler_params=_cparams(("arbitrary",)),
        name="positions",
    )(pstart, idx_t, rank_t)


def _dispatch_kernel(pstart_ref, cnt_ref, pos_ref, h_ref, w1_ref, w3_ref, w2_ref,
                     xs_ref, sh_ref, zrow, sem, zsem, *, td, n_exp, ns):
    step = pl.program_id(0)

    @pl.when(step == 0)
    def _():
        zrow[...] = jnp.zeros_like(zrow)

        def per_expert(e, total):
            lo = pstart_ref[e] + cnt_ref[e]
            hi = pstart_ref[e + 1]

            def fill(r, c):
                _row_copy(zrow, 0, xs_ref, r, zsem, ns).start()
                return c

            lax.fori_loop(lo, hi, fill, 0)
            return total + (hi - lo)

        npad = lax.fori_loop(0, n_exp, per_expert, 0)

        def drain(r, c):
            _row_copy(zrow, 0, xs_ref, 0, zsem, ns).wait()
            return c

        lax.fori_loop(0, npad, drain, 0)

    for t in range(td):
        for k in range(TOP_K):
            _row_copy(h_ref, t, xs_ref, pos_ref[k, t], sem, ns).start(priority=k % 2)

    hk = _load_chunk_bf16(h_ref, 0, td, ns)
    h1 = jnp.dot(hk, w1_ref[0:PAIR, :], preferred_element_type=F32)
    h3 = jnp.dot(hk, w3_ref[0:PAIR, :], preferred_element_type=F32)
    for s in range(1, ns):
        hk = _load_chunk_bf16(h_ref, s, td, ns)
        h1 = h1 + jnp.dot(hk, w1_ref[s * PAIR:(s + 1) * PAIR, :], preferred_element_type=F32)
        h3 = h3 + jnp.dot(hk, w3_ref[s * PAIR:(s + 1) * PAIR, :], preferred_element_type=F32)
    act = (h1 * jax.nn.sigmoid(h1) * h3).astype(BF16)
    sh_ref[...] = jnp.dot(act, w2_ref[...], preferred_element_type=F32)

    for k in range(TOP_K):
        pltpu.make_async_copy(h_ref, xs_ref.at[pl.ds(0, td * ns)], sem).wait()


def _dispatch(pstart, counts, pos_t, h2, w1s, w3s, w2s, n_rows, ns):
    tokens = h2.shape[0] // ns
    n_exp = counts.shape[0]
    d, de = w1s.shape
    td = min(256, tokens)
    kern = functools.partial(_dispatch_kernel, td=td, n_exp=n_exp, ns=ns)
    const2 = lambda i, *_: (0, 0)
    return pl.pallas_call(
        kern,
        out_shape=(jax.ShapeDtypeStruct((n_rows * ns, LANES), U32),
                   jax.ShapeDtypeStruct((tokens, d), F32)),
        grid_spec=pltpu.PrefetchScalarGridSpec(
            num_scalar_prefetch=2,
            grid=(tokens // td,),
            in_specs=[pl.BlockSpec((TOP_K, td), lambda i, *_: (0, i), memory_space=pltpu.SMEM),
                      pl.BlockSpec((td * ns, LANES), lambda i, *_: (i, 0)),
                      _resident((d, de), const2),
                      _resident((d, de), const2),
                      _resident((de, d), const2)],
            out_specs=(pl.BlockSpec(memory_space=pl.ANY),
                       pl.BlockSpec((td, d), lambda i, *_: (i, 0))),
            scratch_shapes=[pltpu.VMEM((ns, LANES), U32),
                            pltpu.SemaphoreType.DMA, pltpu.SemaphoreType.DMA]),
        compiler_params=_cparams(("arbitrary",)),
        name="dispatch",
    )(pstart, counts, pos_t, h2, w1s, w3s, w2s)


def _experts_kernel(bexp_ref, nused_ref, x_ref, w1_ref, w3_ref, w2_ref, y_ref,
                    wb1, wb3, wb2, xb):
    b = pl.program_id(0)
    live = b < nused_ref[0]
    changed = jnp.logical_or(b == 0, bexp_ref[b] != bexp_ref[jnp.maximum(b - 1, 0)])

    @pl.when(jnp.logical_and(live, changed))
    def _():
        wb1[...] = w1_ref[0, 0].astype(BF16)
        wb3[...] = w3_ref[0, 0].astype(BF16)
        wb2[...] = w2_ref[0, 0].astype(BF16)

    @pl.when(live)
    def _():
        ns = wb1.shape[0] // PAIR
        for s in range(ns):
            xb[:, s * PAIR:(s + 1) * PAIR] = _load_chunk_bf16(x_ref, s, EXPERT_BLOCK, ns)
        h1 = jnp.dot(xb[...], wb1[...], preferred_element_type=F32)
        h3 = jnp.dot(xb[...], wb3[...], preferred_element_type=F32)
        act = (h1 * jax.nn.sigmoid(h1) * h3).astype(BF16)
        _store_rows(y_ref, jnp.dot(act, wb2[...], preferred_element_type=F32), EXPERT_BLOCK)


def _experts(block_exp, nused, xs, w1, w3, w2, layer):
    d, de = w1.shape[2], w1.shape[3]
    ns = d // PAIR
    nb = xs.shape[0] // (EXPERT_BLOCK * ns)
    blk = lambda b, be, nu: (jnp.minimum(b, nu[0] - 1), 0)
    wsel = lambda b, be, nu: (layer, be[jnp.minimum(b, nu[0] - 1)], 0, 0)
    return pl.pallas_call(
        _experts_kernel,
        out_shape=jax.ShapeDtypeStruct(xs.shape, U32),
        grid_spec=pltpu.PrefetchScalarGridSpec(
            num_scalar_prefetch=2,
            grid=(nb,),
            in_specs=[pl.BlockSpec((EXPERT_BLOCK * ns, LANES), blk),
                      pl.BlockSpec((1, 1, d, de), wsel),
                      pl.BlockSpec((1, 1, d, de), wsel),
                      pl.BlockSpec((1, 1, de, d), wsel)],
            out_specs=pl.BlockSpec((EXPERT_BLOCK * ns, LANES), blk),
            scratch_shapes=[pltpu.VMEM((d, de), BF16), pltpu.VMEM((d, de), BF16),
                            pltpu.VMEM((de, d), BF16), pltpu.VMEM((EXPERT_BLOCK, d), BF16)]),
        compiler_params=_cparams(("arbitrary",)),
        name="experts",
    )(block_exp, nused, xs, w1, w3, w2)


def _combine_kernel(pos_ref, y_ref, wts_ref, sh_ref, x1_ref, g2_ref, fg_ref, o_ref, buf, sem,
                    *, tc, ns, last_layer):
    for t in range(tc):
        for k in range(TOP_K):
            pltpu.make_async_copy(_row_tile(y_ref, pos_ref[k, t], ns),
                                  _row_tile(buf.at[k], t, ns), sem).start(priority=k % 2)

    for k in range(TOP_K):
        pltpu.make_async_copy(y_ref.at[pl.ds(0, tc * ns)], buf.at[k], sem).wait()

    w = wts_ref[...]
    g2 = g2_ref[0]
    for s in range(ns):
        r_lo, r_hi = _unpack_pair(buf.at[0][pl.ds(s, tc, stride=ns), :])
        r_lo = r_lo * w[:, 0:1]
        r_hi = r_hi * w[:, 0:1]
        for k in range(1, TOP_K):
            y_lo, y_hi = _unpack_pair(buf.at[k][pl.ds(s, tc, stride=ns), :])
            r_lo = r_lo + y_lo * w[:, k:k + 1]
            r_hi = r_hi + y_hi * w[:, k:k + 1]
        c0 = slice(s * PAIR, s * PAIR + LANES)
        c1 = slice(s * PAIR + LANES, (s + 1) * PAIR)
        o_ref[0, :, c0] = x1_ref[0, :, c0] + g2[:, c0] * (r_lo + sh_ref[0, :, c0])
        o_ref[0, :, c1] = x1_ref[0, :, c1] + g2[:, c1] * (r_hi + sh_ref[0, :, c1])
    if last_layer:
        x = o_ref[0]
        ms = jnp.mean(x * x, axis=-1, keepdims=True)
        o_ref[0] = (x * lax.rsqrt(ms + EPS)) * fg_ref[...]


def _combine(pos_t, y, wts_tok, shared, x1, g2, final_g, last_layer):
    bsz, seq, d = x1.shape
    ns = d // PAIR
    tc = min(256, seq)
    nt = seq // tc
    kern = functools.partial(_combine_kernel, tc=tc, ns=ns, last_layer=last_layer)
    row = lambda b, i: (b, i, 0)
    return pl.pallas_call(
        kern,
        out_shape=jax.ShapeDtypeStruct((bsz, seq, d), F32),
        grid=(bsz, nt),
        in_specs=[pl.BlockSpec((TOP_K, tc), lambda b, i: (0, b * nt + i),
                               memory_space=pltpu.SMEM),
                  pl.BlockSpec(memory_space=pl.ANY),
                  pl.BlockSpec((tc, TOP_K), lambda b, i: (b * nt + i, 0)),
                  pl.BlockSpec((1, tc, d), row),
                  pl.BlockSpec((1, tc, d), row),
                  pl.BlockSpec((1, 1, d), lambda b, i: (b, 0, 0)),
                  pl.BlockSpec((1, d), lambda b, i: (0, 0))],
        out_specs=pl.BlockSpec((1, tc, d), row),
        scratch_shapes=[pltpu.VMEM((TOP_K, tc * ns, LANES), U32), pltpu.SemaphoreType.DMA],
        compiler_params=_cparams(("arbitrary", "arbitrary")),
        name="combine",
    )(pos_t, y, wts_tok, shared, x1, g2, final_g)


def _pad_lanes(w, n):
    return jnp.pad(w, ((0, 0), (0, n - w.shape[1])))


def kernel(x, c, norm1_g, norm2_g, w_ada, b_ada, w_in, b_f, w_pool, pool_scale, w_o,
           w_router, router_bias, w1, w3, w2, w1_shared, w3_shared, w2_shared, final_g):
    bsz, seq, d = x.shape
    depth = w_ada.shape[0]
    n_heads = b_f.shape[1]
    da = n_heads * HEAD_DIM
    dp = d - da
    n_exp = w_router.shape[2]
    tokens = bsz * seq
    n_assign = tokens * TOP_K
    n_blocks = -(-n_assign // EXPERT_BLOCK) + n_exp
    n_rows = n_blocks * EXPERT_BLOCK

    mod = _ada(c, w_ada, b_ada)
    mod = mod.reshape(depth, bsz, N_MOD, 1, d)

    for l in range(depth):
        sh1, sc1, g1, sh2, sc2, g2 = (mod[l, :, j] for j in range(N_MOD))
        wl = w_in[l]
        wq = wl[:, :da].astype(BF16)
        wk = wl[:, da:2 * da].astype(BF16)
        wv = wl[:, 2 * da:3 * da].astype(BF16)
        wf = _pad_lanes(wl[:, 3 * da:3 * da + n_heads], LANES).astype(BF16)
        wu = wl[:, 3 * da + n_heads:].astype(BF16)
        bf = _pad_lanes(b_f[l][None, :], LANES)
        q, k, v, dcol, drow, p = _proj(
            x, norm1_g[l][None, :], sh1, sc1, wq, wk, wv, wf, bf, wu,
            w_pool[l].astype(BF16), pool_scale[l][None, :], n_heads=n_heads)
        a = _attn(q, k, v, dcol, drow, n_heads=n_heads)

        wr = _pad_lanes(w_router[l], LANES)
        wr_hi = wr.astype(BF16)
        wr_lo = (wr - wr_hi.astype(F32)).astype(BF16)
        wo = w_o[l].astype(BF16)
        x1, h2, idx_t, wts_t, rank_t, counts = _post(
            a, p, x, g1, norm2_g[l][None, :], sh2, sc2, wo[:da], wo[da:],
            jnp.concatenate([wr_hi, wr_lo], axis=1), wr_hi, router_bias[l][:, None])

        counts = counts[:, 0]
        padded = ((counts + EXPERT_BLOCK - 1) // EXPERT_BLOCK) * EXPERT_BLOCK
        cum = jnp.cumsum(padded)
        pstart = jnp.concatenate([jnp.zeros((1,), I32), cum]).astype(I32)
        nused = (cum[-1:] // EXPERT_BLOCK).astype(I32)
        block_starts = jnp.arange(n_blocks, dtype=I32) * EXPERT_BLOCK
        block_exp = jnp.minimum(
            jnp.sum((cum[None, :] <= block_starts[:, None]).astype(I32), axis=1), n_exp - 1)

        pos_t = _positions(pstart, idx_t, rank_t, n_exp)
        xs, shared = _dispatch(pstart, counts, pos_t, h2, w1_shared[l].astype(BF16),
                               w3_shared[l].astype(BF16), w2_shared[l].astype(BF16),
                               n_rows, d // PAIR)
        y = _experts(block_exp, nused, xs, w1, w3, w2, l)
        x = _combine(pos_t, y, wts_t.T, shared.reshape(bsz, seq, d), x1, g2,
                     final_g[None, :], l == depth - 1)
    return x
```

```python
import functools

import jax
import jax.numpy as jnp
from jax import lax
from jax.experimental import pallas as pl
from jax.experimental.pallas import tpu as pltpu

F32 = jnp.float32
BF16 = jnp.bfloat16
I32 = jnp.int32
U32 = jnp.uint32

EPS = 1e-6
LOG2E = 1.4426950408889634
HEAD_DIM = 128
LANES = 128
POOL_WINDOWS = (2, 4, 8, 16)
POOL_HALO = 16
TOP_K = 8
N_EXPERT_GROUPS = 8
TOPK_GROUPS = 4
ROUTED_SCALE = 2.5
EXPERT_BLOCK = 512
N_MOD = 6
VMEM_LIMIT = 56 * 1024 * 1024

_NT = (((1,), (1,)), ((), ()))


def _cparams(sem):
    return pltpu.CompilerParams(dimension_semantics=sem, vmem_limit_bytes=VMEM_LIMIT)


PAIR = 2 * LANES


def _pack_pair(lo, hi):
    lo = lax.bitcast_convert_type(lo.astype(BF16).astype(F32), U32)
    hi = lax.bitcast_convert_type(hi.astype(BF16).astype(F32), U32)
    return hi | (lo >> 16)


def _unpack_pair(p):
    lo = lax.bitcast_convert_type(p << 16, F32)
    hi = lax.bitcast_convert_type(p & jnp.uint32(0xFFFF0000), F32)
    return lo, hi


def _store_rows(ref, x, m):
    ns = x.shape[1] // PAIR
    for s in range(ns):
        ref[pl.ds(s, m, stride=ns), :] = _pack_pair(x[:, s * PAIR:s * PAIR + LANES],
                                                    x[:, s * PAIR + LANES:(s + 1) * PAIR])


def _load_chunk_bf16(ref, s, m, ns):
    lo, hi = _unpack_pair(ref[pl.ds(s, m, stride=ns), :])
    return jnp.concatenate([lo.astype(BF16), hi.astype(BF16)], axis=1)


def _resident(shape, index_map):
    return pl.BlockSpec(shape, index_map, pipeline_mode=pl.Buffered(1))


def _ada_kernel(c_ref, w_ref, b_ref, o_ref):
    c = c_ref[...]
    ca = c * jax.nn.sigmoid(c)
    o_ref[0] = jnp.dot(ca, w_ref[0], precision=lax.Precision.HIGHEST,
                       preferred_element_type=F32) + b_ref[0]


def _ada(c, w_ada, b_ada):
    depth, d, n = w_ada.shape
    bsz = c.shape[0]
    rows = 8
    cp = jnp.zeros((rows, d), F32).at[:bsz].set(c)
    tn = 512
    out = pl.pallas_call(
        _ada_kernel,
        out_shape=jax.ShapeDtypeStruct((depth, rows, n), F32),
        grid=(depth, n // tn),
        in_specs=[pl.BlockSpec((rows, d), lambda l, j: (0, 0)),
                  pl.BlockSpec((1, d, tn), lambda l, j: (l, 0, j)),
                  pl.BlockSpec((1, 1, tn), lambda l, j: (l, 0, j))],
        out_specs=pl.BlockSpec((1, rows, tn), lambda l, j: (l, 0, j)),
        compiler_params=_cparams(("arbitrary", "arbitrary")),
        name="ada",
    )(cp, w_ada, b_ada.reshape(depth, 1, n))
    return out[:, :bsz]


def _proj_kernel(x_ref, g_ref, sh_ref, sc_ref, wq_ref, wk_ref, wv_ref, wf_ref, bf_ref,
                 wu_ref, wp_ref, ps_ref,
                 q_ref, k_ref, v_ref, dcol_ref, drow_ref, p_ref,
                 ubuf, dcarry, *, tm, n_heads, gc):
    i = pl.program_id(1)
    x = x_ref[0]
    ms = jnp.mean(x * x, axis=-1, keepdims=True)
    y = (x * lax.rsqrt(ms + EPS)) * g_ref[...]
    h = y * (1.0 + sc_ref[0]) + sh_ref[0]
    hb = h.astype(BF16)

    scale = HEAD_DIM ** -0.5 * LOG2E
    q_ref[0] = (jnp.dot(hb, wq_ref[...], preferred_element_type=F32) * scale).astype(BF16)
    k_ref[0] = jnp.dot(hb, wk_ref[...], preferred_element_type=F32).astype(BF16)
    v_ref[0] = jnp.dot(hb, wv_ref[...], preferred_element_type=F32).astype(BF16)

    z = jnp.dot(hb, wf_ref[...], preferred_element_type=F32) + bf_ref[...]
    lf = (jnp.minimum(z, 0.0) - jnp.log1p(jnp.exp(-jnp.abs(z)))) * LOG2E
    r = lax.broadcasted_iota(I32, (tm, tm), 0)
    cidx = lax.broadcasted_iota(I32, (tm, tm), 1)
    tri = (cidx <= r).astype(F32)

    @pl.when(i == 0)
    def _():
        dcarry[...] = jnp.zeros_like(dcarry)
        ubuf[0:POOL_HALO, :] = jnp.zeros((POOL_HALO, ubuf.shape[1]), F32)

    @pl.when(i > 0)
    def _():
        ubuf[0:POOL_HALO, :] = ubuf[tm:tm + POOL_HALO, :]

    dc = jnp.dot(tri, lf, precision=lax.Precision.HIGHEST,
                 preferred_element_type=F32) + dcarry[...]
    dcarry[...] = dc[tm - 1:tm, :]
    dcol_ref[0] = dc
    drow_ref[0] = dc.T[0:drow_ref.shape[1], :]

    u = jnp.dot(hb, wu_ref[...], preferred_element_type=F32)
    ubuf[POOL_HALO:POOL_HALO + tm, :] = u
    pos = i * tm + lax.broadcasted_iota(I32, (tm, 1), 0)
    for g, w in enumerate(POOL_WINDOWS):
        s = ubuf[:, g * gc:(g + 1) * gc]
        step = 1
        while step < w:
            s = s + pltpu.roll(s, step, 0)
            step *= 2
        cnt = jnp.minimum(pos + 1, w).astype(F32)
        pooled = s[POOL_HALO:POOL_HALO + tm, :] / cnt - u[:, g * gc:(g + 1) * gc]
        yg = jnp.dot(pooled.astype(BF16), wp_ref[g], preferred_element_type=F32)
        p_ref[0, :, g * gc:(g + 1) * gc] = (yg * ps_ref[:, g * gc:(g + 1) * gc]).astype(BF16)


def _proj(x, g, sh, sc, wq, wk, wv, wf, bf, wu, wp, ps, *, n_heads):
    bsz, seq, d = x.shape
    da = wq.shape[1]
    dp = wu.shape[1]
    gc = dp // len(POOL_WINDOWS)
    tm = min(512, seq)
    nt = seq // tm
    kern = functools.partial(_proj_kernel, tm=tm, n_heads=n_heads, gc=gc)
    row = lambda b, i: (b, i, 0)
    per_b = lambda b, i: (b, 0, 0)
    const2 = lambda b, i: (0, 0)
    const3 = lambda b, i: (0, 0, 0)
    return pl.pallas_call(
        kern,
        out_shape=(jax.ShapeDtypeStruct((bsz, seq, da), BF16),
                   jax.ShapeDtypeStruct((bsz, seq, da), BF16),
                   jax.ShapeDtypeStruct((bsz, seq, da), BF16),
                   jax.ShapeDtypeStruct((bsz, seq, LANES), F32),
                   jax.ShapeDtypeStruct((bsz, 8, seq), F32),
                   jax.ShapeDtypeStruct((bsz, seq, dp), BF16)),
        grid=(bsz, nt),
        in_specs=[pl.BlockSpec((1, tm, d), row),
                  pl.BlockSpec((1, d), const2),
                  pl.BlockSpec((1, 1, d), per_b),
                  pl.BlockSpec((1, 1, d), per_b),
                  _resident((d, da), const2),
                  _resident((d, da), const2),
                  _resident((d, da), const2),
                  _resident((d, LANES), const2),
                  pl.BlockSpec((1, LANES), const2),
                  _resident((d, dp), const2),
                  _resident(wp.shape, const3),
                  pl.BlockSpec((1, dp), const2)],
        out_specs=(pl.BlockSpec((1, tm, da), row),
                   pl.BlockSpec((1, tm, da), row),
                   pl.BlockSpec((1, tm, da), row),
                   pl.BlockSpec((1, tm, LANES), row),
                   pl.BlockSpec((1, 8, tm), lambda b, i: (b, 0, i)),
                   pl.BlockSpec((1, tm, dp), row)),
        scratch_shapes=[pltpu.VMEM((tm + POOL_HALO, dp), F32),
                        pltpu.VMEM((1, LANES), F32)],
        compiler_params=_cparams(("arbitrary", "arbitrary")),
        name="proj",
    )(x, g, sh, sc, wq, wk, wv, wf, bf, wu, wp, ps)


def _attn_kernel(q_ref, k_ref, v_ref, dcol_ref, drow_ref, o_ref, *, tq, tk, hp):
    hg = pl.program_id(1)
    i = pl.program_id(2)
    lane = lax.broadcasted_iota(I32, (tq, LANES), 1)
    dcol = dcol_ref[0]
    heads = [slice(u * HEAD_DIM, (u + 1) * HEAD_DIM) for u in range(hp)]
    qs = [q_ref[0, :, hs] for hs in heads]
    dqs = [jnp.sum(jnp.where(lane == hg * hp + u, dcol, 0.0), axis=-1, keepdims=True)
           for u in range(hp)]

    def block(j, carry, diag):
        start = pl.multiple_of(j * tk, tk)
        out = []
        for u in range(hp):
            m, l, acc = carry[u]
            ks = k_ref[0, pl.ds(start, tk), heads[u]]
            vs = v_ref[0, pl.ds(start, tk), heads[u]]
            dk = drow_ref[0, pl.ds(hg * hp + u, 1), pl.ds(start, tk)]
            z = lax.dot_general(qs[u], ks, _NT, preferred_element_type=F32) - dk
            if diag is not None:
                rr = lax.broadcasted_iota(I32, (tq, tk), 0)
                cc = lax.broadcasted_iota(I32, (tq, tk), 1)
                z = jnp.where(cc + diag <= rr, z, -jnp.inf)
            m_new = jnp.maximum(m, jnp.max(z, axis=-1, keepdims=True) + dqs[u])
            p = jnp.exp2(z - (m_new - dqs[u]))
            alpha = jnp.exp2(m - m_new)
            l = alpha * l + jnp.sum(p, axis=-1, keepdims=True)
            acc = alpha * acc + jnp.dot(p.astype(BF16), vs, preferred_element_type=F32)
            out.append((m_new, l, acc))
        return tuple(out)

    init = tuple((jnp.full((tq, 1), -jnp.inf, F32), jnp.zeros((tq, 1), F32),
                  jnp.zeros((tq, HEAD_DIM), F32)) for _ in range(hp))
    nin = tq // tk
    carry = lax.fori_loop(0, i * nin, functools.partial(block, diag=None), init)
    for jj in range(nin):
        carry = block(i * nin + jj, carry, jj * tk)
    for u in range(hp):
        _, l, acc = carry[u]
        o_ref[0, :, heads[u]] = (acc / l).astype(o_ref.dtype)


def _attn(q, k, v, dcol, drow, *, n_heads):
    bsz, seq, da = q.shape
    tq = min(1024, seq)
    tk = tq
    nq = seq // tq
    hp = 2 if n_heads % 2 == 0 else 1
    kern = functools.partial(_attn_kernel, tq=tq, tk=tk, hp=hp)
    return pl.pallas_call(
        kern,
        out_shape=jax.ShapeDtypeStruct((bsz, seq, da), BF16),
        grid=(bsz, n_heads // hp, nq),
        in_specs=[pl.BlockSpec((1, tq, hp * HEAD_DIM), lambda b, h, i: (b, i, h)),
                  pl.BlockSpec((1, seq, hp * HEAD_DIM), lambda b, h, i: (b, 0, h)),
                  pl.BlockSpec((1, seq, hp * HEAD_DIM), lambda b, h, i: (b, 0, h)),
                  pl.BlockSpec((1, tq, LANES), lambda b, h, i: (b, i, 0)),
                  pl.BlockSpec((1, 8, seq), lambda b, h, i: (b, 0, 0))],
        out_specs=pl.BlockSpec((1, tq, hp * HEAD_DIM), lambda b, h, i: (b, i, h)),
        compiler_params=_cparams(("arbitrary", "arbitrary", "arbitrary")),
        name="attn",
    )(q, k, v, dcol, drow)


def _post_kernel(a_ref, p_ref, x_ref, g1_ref, ng_ref, sh_ref, sc_ref, woa_ref, wop_ref,
                 wr1_ref, wr2_ref, rb_ref,
                 x1_ref, h2_ref, idx_ref, wts_ref, rank_ref, cnt_ref,
                 carry, *, tm, n_exp):
    first = jnp.logical_and(pl.program_id(0) == 0, pl.program_id(1) == 0)

    @pl.when(first)
    def _():
        carry[...] = jnp.zeros_like(carry)

    mix = jnp.dot(a_ref[0], woa_ref[...], preferred_element_type=F32)
    mix = mix + jnp.dot(p_ref[0], wop_ref[...], preferred_element_type=F32)
    x1 = x_ref[0] + g1_ref[0] * mix
    x1_ref[0] = x1
    ms = jnp.mean(x1 * x1, axis=-1, keepdims=True)
    y = (x1 * lax.rsqrt(ms + EPS)) * ng_ref[...]
    h2 = y * (1.0 + sc_ref[0]) + sh_ref[0]
    _store_rows(h2_ref, h2, tm)

    h_hi = h2.astype(BF16)
    h_lo = (h2 - h_hi.astype(F32)).astype(BF16)
    a1 = jnp.dot(h_hi, wr1_ref[...], preferred_element_type=F32)
    a2 = jnp.dot(h_lo, wr2_ref[...], preferred_element_type=F32)
    logits = a1[:, :LANES] + a1[:, LANES:] + a2
    lt = logits.T[0:n_exp, :]
    sc = jax.nn.sigmoid(lt)
    biased = sc + rb_ref[...]

    epg = n_exp // N_EXPERT_GROUPS
    sub = lax.broadcasted_iota(I32, (epg, tm), 0)
    blocks, gscore = [], []
    for g in range(N_EXPERT_GROUPS):
        blk = biased[g * epg:(g + 1) * epg, :]
        m1 = jnp.max(blk, axis=0, keepdims=True)
        f1 = jnp.min(jnp.where(blk == m1, sub, epg), axis=0, keepdims=True)
        m2 = jnp.max(jnp.where(sub == f1, -jnp.inf, blk), axis=0, keepdims=True)
        blocks.append(blk)
        gscore.append(m1 + m2)
    masked = []
    for g in range(N_EXPERT_GROUPS):
        beats = jnp.zeros((1, tm), I32)
        for g2 in range(N_EXPERT_GROUPS):
            if g2 == g:
                continue
            win = (gscore[g2] > gscore[g]) if g2 > g else (gscore[g2] >= gscore[g])
            beats = beats + win.astype(I32)
        masked.append(jnp.where(beats < TOPK_GROUPS, blocks[g], -jnp.inf))
    v = jnp.concatenate(masked, axis=0)

    eio = lax.broadcasted_iota(I32, (n_exp, tm), 0)
    base = carry[...]
    hits, svals = [], []
    sel = jnp.zeros((n_exp, tm), F32)
    for k in range(TOP_K):
        m = jnp.max(v, axis=0, keepdims=True)
        idx = jnp.min(jnp.where(v == m, eio, n_exp), axis=0, keepdims=True)
        hit = eio == idx
        svals.append(jnp.sum(jnp.where(hit, sc, 0.0), axis=0, keepdims=True))
        v = jnp.where(hit, -jnp.inf, v)
        sel = sel + hit.astype(F32)
        hits.append(hit)
        idx_ref[k:k + 1, :] = idx
    total = svals[0]
    for k in range(1, TOP_K):
        total = total + svals[k]
    for k in range(TOP_K):
        wts_ref[k:k + 1, :] = svals[k] / total * ROUTED_SCALE

    rr = lax.broadcasted_iota(I32, (tm, tm), 0)
    cc = lax.broadcasted_iota(I32, (tm, tm), 1)
    upper = (rr < cc).astype(BF16)
    before = jnp.dot(sel.astype(BF16), upper, preferred_element_type=F32).astype(I32) + base
    for k in range(TOP_K):
        rank_ref[k:k + 1, :] = jnp.sum(jnp.where(hits[k], before, 0), axis=0, keepdims=True)
    new = base + jnp.sum(sel, axis=1, keepdims=True).astype(I32)
    carry[...] = new
    cnt_ref[...] = new


def _post(a, p, x, g1, ng, sh, sc, woa, wop, wr1, wr2, rb):
    bsz, seq, d = x.shape
    da = a.shape[2]
    dp = p.shape[2]
    n_exp = rb.shape[0]
    tm = min(512, seq)
    nt = seq // tm
    tokens = bsz * seq
    kern = functools.partial(_post_kernel, tm=tm, n_exp=n_exp)
    row = lambda b, i: (b, i, 0)
    per_b = lambda b, i: (b, 0, 0)
    const2 = lambda b, i: (0, 0)
    tok = lambda b, i: (0, b * nt + i)
    return pl.pallas_call(
        kern,
        out_shape=(jax.ShapeDtypeStruct((bsz, seq, d), F32),
                   jax.ShapeDtypeStruct((tokens * (d // PAIR), LANES), U32),
                   jax.ShapeDtypeStruct((TOP_K, tokens), I32),
                   jax.ShapeDtypeStruct((TOP_K, tokens), F32),
                   jax.ShapeDtypeStruct((TOP_K, tokens), I32),
                   jax.ShapeDtypeStruct((n_exp, 1), I32)),
        grid=(bsz, nt),
        in_specs=[pl.BlockSpec((1, tm, da), row),
                  pl.BlockSpec((1, tm, dp), row),
                  pl.BlockSpec((1, tm, d), row),
                  pl.BlockSpec((1, 1, d), per_b),
                  pl.BlockSpec((1, d), const2),
                  pl.BlockSpec((1, 1, d), per_b),
                  pl.BlockSpec((1, 1, d), per_b),
                  _resident((da, d), const2),
                  _resident((dp, d), const2),
                  _resident((d, 2 * LANES), const2),
                  _resident((d, LANES), const2),
                  pl.BlockSpec((n_exp, 1), const2)],
        out_specs=(pl.BlockSpec((1, tm, d), row),
                   pl.BlockSpec((tm * (d // PAIR), LANES), lambda b, i: (b * nt + i, 0)),
                   pl.BlockSpec((TOP_K, tm), tok),
                   pl.BlockSpec((TOP_K, tm), tok),
                   pl.BlockSpec((TOP_K, tm), tok),
                   pl.BlockSpec((n_exp, 1), const2)),
        scratch_shapes=[pltpu.VMEM((n_exp, 1), I32)],
        compiler_params=_cparams(("arbitrary", "arbitrary")),
        name="post",
    )(a, p, x, g1, ng, sh, sc, woa, wop, wr1, wr2, rb)


def _row_tile(ref, t, ns):
    start = t * ns if isinstance(t, int) else pl.multiple_of(t * ns, ns)
    return ref.at[pl.ds(start, ns)]


def _row_copy(src, s, dst, t, sem, ns):
    return pltpu.make_async_copy(_row_tile(src, s, ns), _row_tile(dst, t, ns), sem)


def _positions_kernel(pstart_ref, idx_ref, rank_ref, pos_ref, *, n_exp):
    idx = idx_ref[...]
    pos = rank_ref[...]
    for e in range(n_exp):
        pos = pos + jnp.where(idx == e, pstart_ref[e], 0)
    pos_ref[...] = pos


def _positions(pstart, idx_t, rank_t, n_exp):
    tokens = idx_t.shape[1]
    tp = min(4096, tokens)
    blk = pl.BlockSpec((TOP_K, tp), lambda i, *_: (0, i))
    return pl.pallas_call(
        functools.partial(_positions_kernel, n_exp=n_exp),
        out_shape=jax.ShapeDtypeStruct((TOP_K, tokens), I32),
        grid_spec=pltpu.PrefetchScalarGridSpec(
            num_scalar_prefetch=1, grid=(tokens // tp,), in_specs=[blk, blk], out_specs=blk),
        compiler_params=_cparams(("arbitrary",)),
        name="positions",
    )(pstart, idx_t, rank_t)


def _dispatch_kernel(pstart_ref, cnt_ref, pos_ref, h_ref, w1_ref, w3_ref, w2_ref,
                     xs_ref, sh_ref, zrow, sem, zsem, *, td, n_exp, ns):
    step = pl.program_id(0)

    @pl.when(step == 0)
    def _():
        zrow[...] = jnp.zeros_like(zrow)

        def per_expert(e, total):
            lo = pstart_ref[e] + cnt_ref[e]
            hi = pstart_ref[e + 1]

            def fill(r, c):
                _row_copy(zrow, 0, xs_ref, r, zsem, ns).start()
                return c

            lax.fori_loop(lo, hi, fill, 0)
            return total + (hi - lo)

        npad = lax.fori_loop(0, n_exp, per_expert, 0)

        def drain(r, c):
            _row_copy(zrow, 0, xs_ref, 0, zsem, ns).wait()
            return c

        lax.fori_loop(0, npad, drain, 0)

    for t in range(td):
        for k in range(TOP_K):
            _row_copy(h_ref, t, xs_ref, pos_ref[k, t], sem, ns).start(priority=k % 2)

    hk = _load_chunk_bf16(h_ref, 0, td, ns)
    h1 = jnp.dot(hk, w1_ref[0:PAIR, :], preferred_element_type=F32)
    h3 = jnp.dot(hk, w3_ref[0:PAIR, :], preferred_element_type=F32)
    for s in range(1, ns):
        hk = _load_chunk_bf16(h_ref, s, td, ns)
        h1 = h1 + jnp.dot(hk, w1_ref[s * PAIR:(s + 1) * PAIR, :], preferred_element_type=F32)
        h3 = h3 + jnp.dot(hk, w3_ref[s * PAIR:(s + 1) * PAIR, :], preferred_element_type=F32)
    act = (h1 * jax.nn.sigmoid(h1) * h3).astype(BF16)
    sh_ref[...] = jnp.dot(act, w2_ref[...], preferred_element_type=F32)

    for k in range(TOP_K):
        pltpu.make_async_copy(h_ref, xs_ref.at[pl.ds(0, td * ns)], sem).wait()


def _dispatch(pstart, counts, pos_t, h2, w1s, w3s, w2s, n_rows, ns):
    tokens = h2.shape[0] // ns
    n_exp = counts.shape[0]
    d, de = w1s.shape
    td = min(512, tokens)
    kern = functools.partial(_dispatch_kernel, td=td, n_exp=n_exp, ns=ns)
    const2 = lambda i, *_: (0, 0)
    return pl.pallas_call(
        kern,
        out_shape=(jax.ShapeDtypeStruct((n_rows * ns, LANES), U32),
                   jax.ShapeDtypeStruct((tokens, d), F32)),
        grid_spec=pltpu.PrefetchScalarGridSpec(
            num_scalar_prefetch=2,
            grid=(tokens // td,),
            in_specs=[pl.BlockSpec((TOP_K, td), lambda i, *_: (0, i), memory_space=pltpu.SMEM),
                      pl.BlockSpec((td * ns, LANES), lambda i, *_: (i, 0)),
                      _resident((d, de), const2),
                      _resident((d, de), const2),
                      _resident((de, d), const2)],
            out_specs=(pl.BlockSpec(memory_space=pl.ANY),
                       pl.BlockSpec((td, d), lambda i, *_: (i, 0))),
            scratch_shapes=[pltpu.VMEM((ns, LANES), U32),
                            pltpu.SemaphoreType.DMA, pltpu.SemaphoreType.DMA]),
        compiler_params=_cparams(("arbitrary",)),
        name="dispatch",
    )(pstart, counts, pos_t, h2, w1s, w3s, w2s)


def _experts_kernel(bexp_ref, nused_ref, x_ref, w1_ref, w3_ref, w2_ref, y_ref,
                    wb1, wb3, wb2, xb):
    b = pl.program_id(0)
    live = b < nused_ref[0]
    changed = jnp.logical_or(b == 0, bexp_ref[b] != bexp_ref[jnp.maximum(b - 1, 0)])

    @pl.when(jnp.logical_and(live, changed))
    def _():
        wb1[...] = w1_ref[0, 0].astype(BF16)
        wb3[...] = w3_ref[0, 0].astype(BF16)
        wb2[...] = w2_ref[0, 0].astype(BF16)

    @pl.when(live)
    def _():
        ns = wb1.shape[0] // PAIR
        for s in range(ns):
            xb[:, s * PAIR:(s + 1) * PAIR] = _load_chunk_bf16(x_ref, s, EXPERT_BLOCK, ns)
        h1 = jnp.dot(xb[...], wb1[...], preferred_element_type=F32)
        h3 = jnp.dot(xb[...], wb3[...], preferred_element_type=F32)
        act = (h1 * jax.nn.sigmoid(h1) * h3).astype(BF16)
        _store_rows(y_ref, jnp.dot(act, wb2[...], preferred_element_type=F32), EXPERT_BLOCK)


def _experts(block_exp, nused, xs, w1, w3, w2, layer):
    d, de = w1.shape[2], w1.shape[3]
    ns = d // PAIR
    nb = xs.shape[0] // (EXPERT_BLOCK * ns)
    blk = lambda b, be, nu: (jnp.minimum(b, nu[0] - 1), 0)
    wsel = lambda b, be, nu: (layer, be[jnp.minimum(b, nu[0] - 1)], 0, 0)
    return pl.pallas_call(
        _experts_kernel,
        out_shape=jax.ShapeDtypeStruct(xs.shape, U32),
        grid_spec=pltpu.PrefetchScalarGridSpec(
            num_scalar_prefetch=2,
            grid=(nb,),
            in_specs=[pl.BlockSpec((EXPERT_BLOCK * ns, LANES), blk),
                      pl.BlockSpec((1, 1, d, de), wsel),
                      pl.BlockSpec((1, 1, d, de), wsel),
                      pl.BlockSpec((1, 1, de, d), wsel)],
            out_specs=pl.BlockSpec((EXPERT_BLOCK * ns, LANES), blk),
            scratch_shapes=[pltpu.VMEM((d, de), BF16), pltpu.VMEM((d, de), BF16),
                            pltpu.VMEM((de, d), BF16), pltpu.VMEM((EXPERT_BLOCK, d), BF16)]),
        compiler_params=_cparams(("arbitrary",)),
        name="experts",
    )(block_exp, nused, xs, w1, w3, w2)


def _combine_kernel(pos_ref, y_ref, wts_ref, sh_ref, x1_ref, g2_ref, fg_ref, o_ref, buf, sem,
                    *, tc, ns, last_layer):
    for t in range(tc):
        for k in range(TOP_K):
            pltpu.make_async_copy(_row_tile(y_ref, pos_ref[k, t], ns),
                                  _row_tile(buf.at[k], t, ns), sem).start(priority=k % 2)

    for k in range(TOP_K):
        pltpu.make_async_copy(y_ref.at[pl.ds(0, tc * ns)], buf.at[k], sem).wait()

    w = wts_ref[...]
    g2 = g2_ref[0]
    for s in range(ns):
        r_lo, r_hi = _unpack_pair(buf.at[0][pl.ds(s, tc, stride=ns), :])
        r_lo = r_lo * w[:, 0:1]
        r_hi = r_hi * w[:, 0:1]
        for k in range(1, TOP_K):
            y_lo, y_hi = _unpack_pair(buf.at[k][pl.ds(s, tc, stride=ns), :])
            r_lo = r_lo + y_lo * w[:, k:k + 1]
            r_hi = r_hi + y_hi * w[:, k:k + 1]
        c0 = slice(s * PAIR, s * PAIR + LANES)
        c1 = slice(s * PAIR + LANES, (s + 1) * PAIR)
        o_ref[0, :, c0] = x1_ref[0, :, c0] + g2[:, c0] * (r_lo + sh_ref[0, :, c0])
        o_ref[0, :, c1] = x1_ref[0, :, c1] + g2[:, c1] * (r_hi + sh_ref[0, :, c1])
    if last_layer:
        x = o_ref[0]
        ms = jnp.mean(x * x, axis=-1, keepdims=True)
        o_ref[0] = (x * lax.rsqrt(ms + EPS)) * fg_ref[...]


def _combine(pos_t, y, wts_tok, shared, x1, g2, final_g, last_layer):
    bsz, seq, d = x1.shape
    ns = d // PAIR
    tc = min(512, seq)
    nt = seq // tc
    kern = functools.partial(_combine_kernel, tc=tc, ns=ns, last_layer=last_layer)
    row = lambda b, i: (b, i, 0)
    return pl.pallas_call(
        kern,
        out_shape=jax.ShapeDtypeStruct((bsz, seq, d), F32),
        grid=(bsz, nt),
        in_specs=[pl.BlockSpec((TOP_K, tc), lambda b, i: (0, b * nt + i),
                               memory_space=pltpu.SMEM),
                  pl.BlockSpec(memory_space=pl.ANY),
                  pl.BlockSpec((tc, TOP_K), lambda b, i: (b * nt + i, 0)),
                  pl.BlockSpec((1, tc, d), row),
                  pl.BlockSpec((1, tc, d), row),
                  pl.BlockSpec((1, 1, d), lambda b, i: (b, 0, 0)),
                  pl.BlockSpec((1, d), lambda b, i: (0, 0))],
        out_specs=pl.BlockSpec((1, tc, d), row),
        scratch_shapes=[pltpu.VMEM((TOP_K, tc * ns, LANES), U32), pltpu.SemaphoreType.DMA],
        compiler_params=_cparams(("arbitrary", "arbitrary")),
        name="combine",
    )(pos_t, y, wts_tok, shared, x1, g2, final_g)


def _pad_lanes(w, n):
    return jnp.pad(w, ((0, 0), (0, n - w.shape[1])))


def kernel(x, c, norm1_g, norm2_g, w_ada, b_ada, w_in, b_f, w_pool, pool_scale, w_o,
           w_router, router_bias, w1, w3, w2, w1_shared, w3_shared, w2_shared, final_g):
    bsz, seq, d = x.shape
    depth = w_ada.shape[0]
    n_heads = b_f.shape[1]
    da = n_heads * HEAD_DIM
    dp = d - da
    n_exp = w_router.shape[2]
    tokens = bsz * seq
    n_assign = tokens * TOP_K
    n_blocks = -(-n_assign // EXPERT_BLOCK) + n_exp
    n_rows = n_blocks * EXPERT_BLOCK

    mod = _ada(c, w_ada, b_ada)
    mod = mod.reshape(depth, bsz, N_MOD, 1, d)

    for l in range(depth):
        sh1, sc1, g1, sh2, sc2, g2 = (mod[l, :, j] for j in range(N_MOD))
        wl = w_in[l]
        wq = wl[:, :da].astype(BF16)
        wk = wl[:, da:2 * da].astype(BF16)
        wv = wl[:, 2 * da:3 * da].astype(BF16)
        wf = _pad_lanes(wl[:, 3 * da:3 * da + n_heads], LANES).astype(BF16)
        wu = wl[:, 3 * da + n_heads:].astype(BF16)
        bf = _pad_lanes(b_f[l][None, :], LANES)
        q, k, v, dcol, drow, p = _proj(
            x, norm1_g[l][None, :], sh1, sc1, wq, wk, wv, wf, bf, wu,
            w_pool[l].astype(BF16), pool_scale[l][None, :], n_heads=n_heads)
        a = _attn(q, k, v, dcol, drow, n_heads=n_heads)

        wr = _pad_lanes(w_router[l], LANES)
        wr_hi = wr.astype(BF16)
        wr_lo = (wr - wr_hi.astype(F32)).astype(BF16)
        wo = w_o[l].astype(BF16)
        x1, h2, idx_t, wts_t, rank_t, counts = _post(
            a, p, x, g1, norm2_g[l][None, :], sh2, sc2, wo[:da], wo[da:],
            jnp.concatenate([wr_hi, wr_lo], axis=1), wr_hi, router_bias[l][:, None])

        counts = counts[:, 0]
        padded = ((counts + EXPERT_BLOCK - 1) // EXPERT_BLOCK) * EXPERT_BLOCK
        cum = jnp.cumsum(padded)
        pstart = jnp.concatenate([jnp.zeros((1,), I32), cum]).astype(I32)
        nused = (cum[-1:] // EXPERT_BLOCK).astype(I32)
        block_starts = jnp.arange(n_blocks, dtype=I32) * EXPERT_BLOCK
        block_exp = jnp.minimum(
            jnp.sum((cum[None, :] <= block_starts[:, None]).astype(I32), axis=1), n_exp - 1)

        pos_t = _positions(pstart, idx_t, rank_t, n_exp)
        xs, shared = _dispatch(pstart, counts, pos_t, h2, w1_shared[l].astype(BF16),
                               w3_shared[l].astype(BF16), w2_shared[l].astype(BF16),
                               n_rows, d // PAIR)
        y = _experts(block_exp, nused, xs, w1, w3, w2, l)
        x = _combine(pos_t, y, wts_t.T, shared.reshape(bsz, seq, d), x1, g2,
                     final_g[None, :], l == depth - 1)
    return x
```

```python
import functools

import jax
import jax.numpy as jnp
from jax import lax
from jax.experimental import pallas as pl
from jax.experimental.pallas import tpu as pltpu

F32 = jnp.float32
BF16 = jnp.bfloat16
I32 = jnp.int32
U32 = jnp.uint32

EPS = 1e-6
LOG2E = 1.4426950408889634
HEAD_DIM = 128
LANES = 128
POOL_WINDOWS = (2, 4, 8, 16)
POOL_HALO = 16
TOP_K = 8
N_EXPERT_GROUPS = 8
TOPK_GROUPS = 4
ROUTED_SCALE = 2.5
EXPERT_BLOCK = 512
N_MOD = 6
VMEM_LIMIT = 56 * 1024 * 1024

_NT = (((1,), (1,)), ((), ()))


def _cparams(sem):
    return pltpu.CompilerParams(dimension_semantics=sem, vmem_limit_bytes=VMEM_LIMIT)


PAIR = 2 * LANES


def _pack_pair(lo, hi):
    lo = lax.bitcast_convert_type(lo.astype(BF16).astype(F32), U32)
    hi = lax.bitcast_convert_type(hi.astype(BF16).astype(F32), U32)
    return hi | (lo >> 16)


def _unpack_pair(p):
    lo = lax.bitcast_convert_type(p << 16, F32)
    hi = lax.bitcast_convert_type(p & jnp.uint32(0xFFFF0000), F32)
    return lo, hi


def _store_rows(ref, x, m):
    ns = x.shape[1] // PAIR
    for s in range(ns):
        ref[pl.ds(s, m, stride=ns), :] = _pack_pair(x[:, s * PAIR:s * PAIR + LANES],
                                                    x[:, s * PAIR + LANES:(s + 1) * PAIR])


def _load_chunk_bf16(ref, s, m, ns):
    lo, hi = _unpack_pair(ref[pl.ds(s, m, stride=ns), :])
    return jnp.concatenate([lo.astype(BF16), hi.astype(BF16)], axis=1)


def _resident(shape, index_map):
    return pl.BlockSpec(shape, index_map, pipeline_mode=pl.Buffered(1))


def _ada_kernel(c_ref, w_ref, b_ref, o_ref):
    c = c_ref[...]
    ca = c * jax.nn.sigmoid(c)
    o_ref[0] = jnp.dot(ca, w_ref[0], precision=lax.Precision.HIGHEST,
                       preferred_element_type=F32) + b_ref[0]


def _ada(c, w_ada, b_ada):
    depth, d, n = w_ada.shape
    bsz = c.shape[0]
    rows = 8
    cp = jnp.zeros((rows, d), F32).at[:bsz].set(c)
    tn = 1536 if n % 1536 == 0 else 512
    out = pl.pallas_call(
        _ada_kernel,
        out_shape=jax.ShapeDtypeStruct((depth, rows, n), F32),
        grid=(depth, n // tn),
        in_specs=[pl.BlockSpec((rows, d), lambda l, j: (0, 0)),
                  pl.BlockSpec((1, d, tn), lambda l, j: (l, 0, j)),
                  pl.BlockSpec((1, 1, tn), lambda l, j: (l, 0, j))],
        out_specs=pl.BlockSpec((1, rows, tn), lambda l, j: (l, 0, j)),
        compiler_params=_cparams(("arbitrary", "arbitrary")),
        name="ada",
    )(cp, w_ada, b_ada.reshape(depth, 1, n))
    return out[:, :bsz]


def _proj_kernel(x_ref, g_ref, sh_ref, sc_ref, wq_ref, wk_ref, wv_ref, wf_ref, bf_ref,
                 wu_ref, wp_ref, ps_ref,
                 q_ref, k_ref, v_ref, dcol_ref, drow_ref, p_ref,
                 ubuf, dcarry, *, tm, n_heads, gc):
    i = pl.program_id(1)
    x = x_ref[0]
    ms = jnp.mean(x * x, axis=-1, keepdims=True)
    y = (x * lax.rsqrt(ms + EPS)) * g_ref[...]
    h = y * (1.0 + sc_ref[0]) + sh_ref[0]
    hb = h.astype(BF16)

    scale = HEAD_DIM ** -0.5 * LOG2E
    q_ref[0] = (jnp.dot(hb, wq_ref[...], preferred_element_type=F32) * scale).astype(BF16)
    k_ref[0] = jnp.dot(hb, wk_ref[...], preferred_element_type=F32).astype(BF16)
    v_ref[0] = jnp.dot(hb, wv_ref[...], preferred_element_type=F32).astype(BF16)

    z = jnp.dot(hb, wf_ref[...], preferred_element_type=F32) + bf_ref[...]
    lf = (jnp.minimum(z, 0.0) - jnp.log1p(jnp.exp(-jnp.abs(z)))) * LOG2E

    @pl.when(i == 0)
    def _():
        dcarry[...] = jnp.zeros_like(dcarry)
        ubuf[0:POOL_HALO, :] = jnp.zeros((POOL_HALO, ubuf.shape[1]), F32)

    @pl.when(i > 0)
    def _():
        ubuf[0:POOL_HALO, :] = ubuf[tm:tm + POOL_HALO, :]

    row = lax.broadcasted_iota(I32, (tm, LANES), 0)
    dc = lf
    span = 1
    while span < tm:
        dc = dc + jnp.where(row >= span, pltpu.roll(dc, span, 0), 0.0)
        span *= 2
    dc = dc + dcarry[...]
    dcarry[...] = dc[tm - 1:tm, :]
    dcol_ref[0] = dc
    drow_ref[0] = dc.T[0:drow_ref.shape[1], :]

    u = jnp.dot(hb, wu_ref[...], preferred_element_type=F32)
    ubuf[POOL_HALO:POOL_HALO + tm, :] = u
    pos = i * tm + lax.broadcasted_iota(I32, (tm, 1), 0)
    for g, w in enumerate(POOL_WINDOWS):
        s = ubuf[:, g * gc:(g + 1) * gc]
        step = 1
        while step < w:
            s = s + pltpu.roll(s, step, 0)
            step *= 2
        cnt = jnp.minimum(pos + 1, w).astype(F32)
        pooled = s[POOL_HALO:POOL_HALO + tm, :] / cnt - u[:, g * gc:(g + 1) * gc]
        yg = jnp.dot(pooled.astype(BF16), wp_ref[g], preferred_element_type=F32)
        p_ref[0, :, g * gc:(g + 1) * gc] = (yg * ps_ref[:, g * gc:(g + 1) * gc]).astype(BF16)


def _proj(x, g, sh, sc, wq, wk, wv, wf, bf, wu, wp, ps, *, n_heads):
    bsz, seq, d = x.shape
    da = wq.shape[1]
    dp = wu.shape[1]
    gc = dp // len(POOL_WINDOWS)
    tm = min(512, seq)
    nt = seq // tm
    kern = functools.partial(_proj_kernel, tm=tm, n_heads=n_heads, gc=gc)
    row = lambda b, i: (b, i, 0)
    per_b = lambda b, i: (b, 0, 0)
    const2 = lambda b, i: (0, 0)
    const3 = lambda b, i: (0, 0, 0)
    return pl.pallas_call(
        kern,
        out_shape=(jax.ShapeDtypeStruct((bsz, seq, da), BF16),
                   jax.ShapeDtypeStruct((bsz, seq, da), BF16),
                   jax.ShapeDtypeStruct((bsz, seq, da), BF16),
                   jax.ShapeDtypeStruct((bsz, seq, LANES), F32),
                   jax.ShapeDtypeStruct((bsz, 8, seq), F32),
                   jax.ShapeDtypeStruct((bsz, seq, dp), BF16)),
        grid=(bsz, nt),
        in_specs=[pl.BlockSpec((1, tm, d), row),
                  pl.BlockSpec((1, d), const2),
                  pl.BlockSpec((1, 1, d), per_b),
                  pl.BlockSpec((1, 1, d), per_b),
                  _resident((d, da), const2),
                  _resident((d, da), const2),
                  _resident((d, da), const2),
                  _resident((d, LANES), const2),
                  pl.BlockSpec((1, LANES), const2),
                  _resident((d, dp), const2),
                  _resident(wp.shape, const3),
                  pl.BlockSpec((1, dp), const2)],
        out_specs=(pl.BlockSpec((1, tm, da), row),
                   pl.BlockSpec((1, tm, da), row),
                   pl.BlockSpec((1, tm, da), row),
                   pl.BlockSpec((1, tm, LANES), row),
                   pl.BlockSpec((1, 8, tm), lambda b, i: (b, 0, i)),
                   pl.BlockSpec((1, tm, dp), row)),
        scratch_shapes=[pltpu.VMEM((tm + POOL_HALO, dp), F32),
                        pltpu.VMEM((1, LANES), F32)],
        compiler_params=_cparams(("arbitrary", "arbitrary")),
        name="proj",
    )(x, g, sh, sc, wq, wk, wv, wf, bf, wu, wp, ps)


def _attn_kernel(q_ref, k_ref, v_ref, dcol_ref, drow_ref, o_ref, *, tq, tk, hp):
    hg = pl.program_id(1)
    i = pl.program_id(2)
    lane = lax.broadcasted_iota(I32, (tq, LANES), 1)
    dcol = dcol_ref[0]
    heads = [slice(u * HEAD_DIM, (u + 1) * HEAD_DIM) for u in range(hp)]
    qs = [q_ref[0, :, hs] for hs in heads]
    dqs = [jnp.sum(jnp.where(lane == hg * hp + u, dcol, 0.0), axis=-1, keepdims=True)
           for u in range(hp)]

    def block(j, carry, diag):
        start = pl.multiple_of(j * tk, tk)
        out = []
        for u in range(hp):
            m, l, acc = carry[u]
            ks = k_ref[0, pl.ds(start, tk), heads[u]]
            vs = v_ref[0, pl.ds(start, tk), heads[u]]
            dk = drow_ref[0, pl.ds(hg * hp + u, 1), pl.ds(start, tk)]
            z = lax.dot_general(qs[u], ks, _NT, preferred_element_type=F32) - dk
            if diag is not None:
                rr = lax.broadcasted_iota(I32, (tq, tk), 0)
                cc = lax.broadcasted_iota(I32, (tq, tk), 1)
                z = jnp.where(cc + diag <= rr, z, -jnp.inf)
            m_new = jnp.maximum(m, jnp.max(z, axis=-1, keepdims=True) + dqs[u])
            p = jnp.exp2(z - (m_new - dqs[u]))
            alpha = jnp.exp2(m - m_new)
            l = alpha * l + jnp.sum(p, axis=-1, keepdims=True)
            acc = alpha * acc + jnp.dot(p.astype(BF16), vs, preferred_element_type=F32)
            out.append((m_new, l, acc))
        return tuple(out)

    init = tuple((jnp.full((tq, 1), -jnp.inf, F32), jnp.zeros((tq, 1), F32),
                  jnp.zeros((tq, HEAD_DIM), F32)) for _ in range(hp))
    nin = tq // tk
    carry = lax.fori_loop(0, i * nin, functools.partial(block, diag=None), init)
    for jj in range(nin):
        carry = block(i * nin + jj, carry, jj * tk)
    for u in range(hp):
        _, l, acc = carry[u]
        o_ref[0, :, heads[u]] = (acc / l).astype(o_ref.dtype)


def _attn(q, k, v, dcol, drow, *, n_heads):
    bsz, seq, da = q.shape
    tq = min(1024, seq)
    tk = tq
    nq = seq // tq
    hp = 2 if n_heads % 2 == 0 else 1
    kern = functools.partial(_attn_kernel, tq=tq, tk=tk, hp=hp)
    return pl.pallas_call(
        kern,
        out_shape=jax.ShapeDtypeStruct((bsz, seq, da), BF16),
        grid=(bsz, n_heads // hp, nq),
        in_specs=[pl.BlockSpec((1, tq, hp * HEAD_DIM), lambda b, h, i: (b, i, h)),
                  pl.BlockSpec((1, seq, hp * HEAD_DIM), lambda b, h, i: (b, 0, h)),
                  pl.BlockSpec((1, seq, hp * HEAD_DIM), lambda b, h, i: (b, 0, h)),
                  pl.BlockSpec((1, tq, LANES), lambda b, h, i: (b, i, 0)),
                  pl.BlockSpec((1, 8, seq), lambda b, h, i: (b, 0, 0))],
        out_specs=pl.BlockSpec((1, tq, hp * HEAD_DIM), lambda b, h, i: (b, i, h)),
        compiler_params=_cparams(("arbitrary", "arbitrary", "arbitrary")),
        name="attn",
    )(q, k, v, dcol, drow)


def _post_kernel(a_ref, p_ref, x_ref, g1_ref, ng_ref, sh_ref, sc_ref, woa_ref, wop_ref,
                 wr1_ref, wr2_ref, rb_ref,
                 x1_ref, h2_ref, idx_ref, wts_ref, rank_ref, cnt_ref,
                 carry, *, tm, n_exp):
    first = jnp.logical_and(pl.program_id(0) == 0, pl.program_id(1) == 0)

    @pl.when(first)
    def _():
        carry[...] = jnp.zeros_like(carry)

    mix = jnp.dot(a_ref[0], woa_ref[...], preferred_element_type=F32)
    mix = mix + jnp.dot(p_ref[0], wop_ref[...], preferred_element_type=F32)
    x1 = x_ref[0] + g1_ref[0] * mix
    x1_ref[0] = x1
    ms = jnp.mean(x1 * x1, axis=-1, keepdims=True)
    y = (x1 * lax.rsqrt(ms + EPS)) * ng_ref[...]
    h2 = y * (1.0 + sc_ref[0]) + sh_ref[0]
    _store_rows(h2_ref, h2, tm)

    h_hi = h2.astype(BF16)
    h_lo = (h2 - h_hi.astype(F32)).astype(BF16)
    a1 = jnp.dot(h_hi, wr1_ref[...], preferred_element_type=F32)
    a2 = jnp.dot(h_lo, wr2_ref[...], preferred_element_type=F32)
    logits = a1[:, :LANES] + a1[:, LANES:] + a2
    lt = logits.T[0:n_exp, :]
    sc = jax.nn.sigmoid(lt)
    biased = sc + rb_ref[...]

    epg = n_exp // N_EXPERT_GROUPS
    sub = lax.broadcasted_iota(I32, (epg, tm), 0)
    blocks, gscore = [], []
    for g in range(N_EXPERT_GROUPS):
        blk = biased[g * epg:(g + 1) * epg, :]
        m1 = jnp.max(blk, axis=0, keepdims=True)
        f1 = jnp.min(jnp.where(blk == m1, sub, epg), axis=0, keepdims=True)
        m2 = jnp.max(jnp.where(sub == f1, -jnp.inf, blk), axis=0, keepdims=True)
        blocks.append(blk)
        gscore.append(m1 + m2)
    masked = []
    for g in range(N_EXPERT_GROUPS):
        beats = jnp.zeros((1, tm), I32)
        for g2 in range(N_EXPERT_GROUPS):
            if g2 == g:
                continue
            win = (gscore[g2] > gscore[g]) if g2 > g else (gscore[g2] >= gscore[g])
            beats = beats + win.astype(I32)
        masked.append(jnp.where(beats < TOPK_GROUPS, blocks[g], -jnp.inf))
    v = jnp.concatenate(masked, axis=0)

    eio = lax.broadcasted_iota(I32, (n_exp, tm), 0)
    base = carry[...]
    hits, svals = [], []
    sel = jnp.zeros((n_exp, tm), F32)
    for k in range(TOP_K):
        m = jnp.max(v, axis=0, keepdims=True)
        idx = jnp.min(jnp.where(v == m, eio, n_exp), axis=0, keepdims=True)
        hit = eio == idx
        svals.append(jnp.sum(jnp.where(hit, sc, 0.0), axis=0, keepdims=True))
        v = jnp.where(hit, -jnp.inf, v)
        sel = sel + hit.astype(F32)
        hits.append(hit)
        idx_ref[k:k + 1, :] = idx
    total = svals[0]
    for k in range(1, TOP_K):
        total = total + svals[k]
    for k in range(TOP_K):
        wts_ref[k:k + 1, :] = svals[k] / total * ROUTED_SCALE

    rr = lax.broadcasted_iota(I32, (tm, tm), 0)
    cc = lax.broadcasted_iota(I32, (tm, tm), 1)
    upper = (rr < cc).astype(BF16)
    before = jnp.dot(sel.astype(BF16), upper, preferred_element_type=F32).astype(I32) + base
    for k in range(TOP_K):
        rank_ref[k:k + 1, :] = jnp.sum(jnp.where(hits[k], before, 0), axis=0, keepdims=True)
    new = base + jnp.sum(sel, axis=1, keepdims=True).astype(I32)
    carry[...] = new
    cnt_ref[...] = new


def _post(a, p, x, g1, ng, sh, sc, woa, wop, wr1, wr2, rb):
    bsz, seq, d = x.shape
    da = a.shape[2]
    dp = p.shape[2]
    n_exp = rb.shape[0]
    tm = min(512, seq)
    nt = seq // tm
    tokens = bsz * seq
    kern = functools.partial(_post_kernel, tm=tm, n_exp=n_exp)
    row = lambda b, i: (b, i, 0)
    per_b = lambda b, i: (b, 0, 0)
    const2 = lambda b, i: (0, 0)
    tok = lambda b, i: (0, b * nt + i)
    return pl.pallas_call(
        kern,
        out_shape=(jax.ShapeDtypeStruct((bsz, seq, d), F32),
                   jax.ShapeDtypeStruct((tokens * (d // PAIR), LANES), U32),
                   jax.ShapeDtypeStruct((TOP_K, tokens), I32),
                   jax.ShapeDtypeStruct((TOP_K, tokens), F32),
                   jax.ShapeDtypeStruct((TOP_K, tokens), I32),
                   jax.ShapeDtypeStruct((n_exp, 1), I32)),
        grid=(bsz, nt),
        in_specs=[pl.BlockSpec((1, tm, da), row),
                  pl.BlockSpec((1, tm, dp), row),
                  pl.BlockSpec((1, tm, d), row),
                  pl.BlockSpec((1, 1, d), per_b),
                  pl.BlockSpec((1, d), const2),
                  pl.BlockSpec((1, 1, d), per_b),
                  pl.BlockSpec((1, 1, d), per_b),
                  _resident((da, d), const2),
                  _resident((dp, d), const2),
                  _resident((d, 2 * LANES), const2),
                  _resident((d, LANES), const2),
                  pl.BlockSpec((n_exp, 1), const2)],
        out_specs=(pl.BlockSpec((1, tm, d), row),
                   pl.BlockSpec((tm * (d // PAIR), LANES), lambda b, i: (b * nt + i, 0)),
                   pl.BlockSpec((TOP_K, tm), tok),
                   pl.BlockSpec((TOP_K, tm), tok),
                   pl.BlockSpec((TOP_K, tm), tok),
                   pl.BlockSpec((n_exp, 1), const2)),
        scratch_shapes=[pltpu.VMEM((n_exp, 1), I32)],
        compiler_params=_cparams(("arbitrary", "arbitrary")),
        name="post",
    )(a, p, x, g1, ng, sh, sc, woa, wop, wr1, wr2, rb)


def _row_tile(ref, t, ns):
    start = t * ns if isinstance(t, int) else pl.multiple_of(t * ns, ns)
    return ref.at[pl.ds(start, ns)]


def _row_copy(src, s, dst, t, sem, ns):
    return pltpu.make_async_copy(_row_tile(src, s, ns), _row_tile(dst, t, ns), sem)


def _positions_kernel(pstart_ref, idx_ref, rank_ref, pos_ref, *, n_exp):
    idx = idx_ref[...]
    pos = rank_ref[...]
    for e in range(n_exp):
        pos = pos + jnp.where(idx == e, pstart_ref[e], 0)
    pos_ref[...] = pos


def _positions(pstart, idx_t, rank_t, n_exp):
    tokens = idx_t.shape[1]
    tp = min(4096, tokens)
    blk = pl.BlockSpec((TOP_K, tp), lambda i, *_: (0, i))
    return pl.pallas_call(
        functools.partial(_positions_kernel, n_exp=n_exp),
        out_shape=jax.ShapeDtypeStruct((TOP_K, tokens), I32),
        grid_spec=pltpu.PrefetchScalarGridSpec(
            num_scalar_prefetch=1, grid=(tokens // tp,), in_specs=[blk, blk], out_specs=blk),
        compiler_params=_cparams(("arbitrary",)),
        name="positions",
    )(pstart, idx_t, rank_t)


def _dispatch_kernel(pstart_ref, cnt_ref, pos_ref, h_ref, w1_ref, w3_ref, w2_ref,
                     xs_ref, sh_ref, zrow, sem, zsem, *, td, n_exp, ns):
    step = pl.program_id(0)

    @pl.when(step == 0)
    def _():
        zrow[...] = jnp.zeros_like(zrow)

        def per_expert(e, total):
            lo = pstart_ref[e] + cnt_ref[e]
            hi = pstart_ref[e + 1]

            def fill(r, c):
                _row_copy(zrow, 0, xs_ref, r, zsem, ns).start()
                return c

            lax.fori_loop(lo, hi, fill, 0)
            return total + (hi - lo)

        npad = lax.fori_loop(0, n_exp, per_expert, 0)

        def drain(r, c):
            _row_copy(zrow, 0, xs_ref, 0, zsem, ns).wait()
            return c

        lax.fori_loop(0, npad, drain, 0)

    for t in range(td):
        for k in range(TOP_K):
            _row_copy(h_ref, t, xs_ref, pos_ref[k, t], sem, ns).start(priority=k % 2)

    hk = _load_chunk_bf16(h_ref, 0, td, ns)
    h1 = jnp.dot(hk, w1_ref[0:PAIR, :], preferred_element_type=F32)
    h3 = jnp.dot(hk, w3_ref[0:PAIR, :], preferred_element_type=F32)
    for s in range(1, ns):
        hk = _load_chunk_bf16(h_ref, s, td, ns)
        h1 = h1 + jnp.dot(hk, w1_ref[s * PAIR:(s + 1) * PAIR, :], preferred_element_type=F32)
        h3 = h3 + jnp.dot(hk, w3_ref[s * PAIR:(s + 1) * PAIR, :], preferred_element_type=F32)
    act = (h1 * jax.nn.sigmoid(h1) * h3).astype(BF16)
    sh_ref[...] = jnp.dot(act, w2_ref[...], preferred_element_type=F32)

    for k in range(TOP_K):
        pltpu.make_async_copy(h_ref, xs_ref.at[pl.ds(0, td * ns)], sem).wait()


def _dispatch(pstart, counts, pos_t, h2, w1s, w3s, w2s, n_rows, ns):
    tokens = h2.shape[0] // ns
    n_exp = counts.shape[0]
    d, de = w1s.shape
    td = min(512, tokens)
    kern = functools.partial(_dispatch_kernel, td=td, n_exp=n_exp, ns=ns)
    const2 = lambda i, *_: (0, 0)
    return pl.pallas_call(
        kern,
        out_shape=(jax.ShapeDtypeStruct((n_rows * ns, LANES), U32),
                   jax.ShapeDtypeStruct((tokens, d), F32)),
        grid_spec=pltpu.PrefetchScalarGridSpec(
            num_scalar_prefetch=2,
            grid=(tokens // td,),
            in_specs=[pl.BlockSpec((TOP_K, td), lambda i, *_: (0, i), memory_space=pltpu.SMEM),
                      pl.BlockSpec((td * ns, LANES), lambda i, *_: (i, 0)),
                      _resident((d, de), const2),
                      _resident((d, de), const2),
                      _resident((de, d), const2)],
            out_specs=(pl.BlockSpec(memory_space=pl.ANY),
                       pl.BlockSpec((td, d), lambda i, *_: (i, 0))),
            scratch_shapes=[pltpu.VMEM((ns, LANES), U32),
                            pltpu.SemaphoreType.DMA, pltpu.SemaphoreType.DMA]),
        compiler_params=_cparams(("arbitrary",)),
        name="dispatch",
    )(pstart, counts, pos_t, h2, w1s, w3s, w2s)


def _experts_kernel(bexp_ref, nused_ref, x_ref, w1_ref, w3_ref, w2_ref, y_ref,
                    wb1, wb3, wb2, xb):
    b = pl.program_id(0)
    live = b < nused_ref[0]
    changed = jnp.logical_or(b == 0, bexp_ref[b] != bexp_ref[jnp.maximum(b - 1, 0)])

    @pl.when(jnp.logical_and(live, changed))
    def _():
        wb1[...] = w1_ref[0, 0].astype(BF16)
        wb3[...] = w3_ref[0, 0].astype(BF16)
        wb2[...] = w2_ref[0, 0].astype(BF16)

    @pl.when(live)
    def _():
        ns = wb1.shape[0] // PAIR
        for s in range(ns):
            xb[:, s * PAIR:(s + 1) * PAIR] = _load_chunk_bf16(x_ref, s, EXPERT_BLOCK, ns)
        h1 = jnp.dot(xb[...], wb1[...], preferred_element_type=F32)
        h3 = jnp.dot(xb[...], wb3[...], preferred_element_type=F32)
        act = (h1 * jax.nn.sigmoid(h1) * h3).astype(BF16)
        _store_rows(y_ref, jnp.dot(act, wb2[...], preferred_element_type=F32), EXPERT_BLOCK)


def _experts(block_exp, nused, xs, w1, w3, w2, layer):
    d, de = w1.shape[2], w1.shape[3]
    ns = d // PAIR
    nb = xs.shape[0] // (EXPERT_BLOCK * ns)
    blk = lambda b, be, nu: (jnp.minimum(b, nu[0] - 1), 0)
    wsel = lambda b, be, nu: (layer, be[jnp.minimum(b, nu[0] - 1)], 0, 0)
    return pl.pallas_call(
        _experts_kernel,
        out_shape=jax.ShapeDtypeStruct(xs.shape, U32),
        grid_spec=pltpu.PrefetchScalarGridSpec(
            num_scalar_prefetch=2,
            grid=(nb,),
            in_specs=[pl.BlockSpec((EXPERT_BLOCK * ns, LANES), blk),
                      pl.BlockSpec((1, 1, d, de), wsel),
                      pl.BlockSpec((1, 1, d, de), wsel),
                      pl.BlockSpec((1, 1, de, d), wsel)],
            out_specs=pl.BlockSpec((EXPERT_BLOCK * ns, LANES), blk),
            scratch_shapes=[pltpu.VMEM((d, de), BF16), pltpu.VMEM((d, de), BF16),
                            pltpu.VMEM((de, d), BF16), pltpu.VMEM((EXPERT_BLOCK, d), BF16)]),
        compiler_params=_cparams(("arbitrary",)),
        name="experts",
    )(block_exp, nused, xs, w1, w3, w2)


def _combine_kernel(pos_ref, y_ref, wts_ref, sh_ref, x1_ref, g2_ref, fg_ref, o_ref, buf, sem,
                    *, tc, ns, last_layer):
    for t in range(tc):
        for k in range(TOP_K):
            pltpu.make_async_copy(_row_tile(y_ref, pos_ref[k, t], ns),
                                  _row_tile(buf.at[k], t, ns), sem).start(priority=k % 2)

    for k in range(TOP_K):
        pltpu.make_async_copy(y_ref.at[pl.ds(0, tc * ns)], buf.at[k], sem).wait()

    w = wts_ref[...]
    g2 = g2_ref[0]
    for s in range(ns):
        r_lo, r_hi = _unpack_pair(buf.at[0][pl.ds(s, tc, stride=ns), :])
        r_lo = r_lo * w[:, 0:1]
        r_hi = r_hi * w[:, 0:1]
        for k in range(1, TOP_K):
            y_lo, y_hi = _unpack_pair(buf.at[k][pl.ds(s, tc, stride=ns), :])
            r_lo = r_lo + y_lo * w[:, k:k + 1]
            r_hi = r_hi + y_hi * w[:, k:k + 1]
        c0 = slice(s * PAIR, s * PAIR + LANES)
        c1 = slice(s * PAIR + LANES, (s + 1) * PAIR)
        o_ref[0, :, c0] = x1_ref[0, :, c0] + g2[:, c0] * (r_lo + sh_ref[0, :, c0])
        o_ref[0, :, c1] = x1_ref[0, :, c1] + g2[:, c1] * (r_hi + sh_ref[0, :, c1])
    if last_layer:
        x = o_ref[0]
        ms = jnp.mean(x * x, axis=-1, keepdims=True)
        o_ref[0] = (x * lax.rsqrt(ms + EPS)) * fg_ref[...]


def _combine(pos_t, y, wts_tok, shared, x1, g2, final_g, last_layer):
    bsz, seq, d = x1.shape
    ns = d // PAIR
    tc = min(512, seq)
    nt = seq // tc
    kern = functools.partial(_combine_kernel, tc=tc, ns=ns, last_layer=last_layer)
    row = lambda b, i: (b, i, 0)
    return pl.pallas_call(
        kern,
        out_shape=jax.ShapeDtypeStruct((bsz, seq, d), F32),
        grid=(bsz, nt),
        in_specs=[pl.BlockSpec((TOP_K, tc), lambda b, i: (0, b * nt + i),
                               memory_space=pltpu.SMEM),
                  pl.BlockSpec(memory_space=pl.ANY),
                  pl.BlockSpec((tc, TOP_K), lambda b, i: (b * nt + i, 0)),
                  pl.BlockSpec((1, tc, d), row),
                  pl.BlockSpec((1, tc, d), row),
                  pl.BlockSpec((1, 1, d), lambda b, i: (b, 0, 0)),
                  pl.BlockSpec((1, d), lambda b, i: (0, 0))],
        out_specs=pl.BlockSpec((1, tc, d), row),
        scratch_shapes=[pltpu.VMEM((TOP_K, tc * ns, LANES), U32), pltpu.SemaphoreType.DMA],
        compiler_params=_cparams(("arbitrary", "arbitrary")),
        name="combine",
    )(pos_t, y, wts_tok, shared, x1, g2, final_g)


def _pad_lanes(w, n):
    return jnp.pad(w, ((0, 0), (0, n - w.shape[1])))


def kernel(x, c, norm1_g, norm2_g, w_ada, b_ada, w_in, b_f, w_pool, pool_scale, w_o,
           w_router, router_bias, w1, w3, w2, w1_shared, w3_shared, w2_shared, final_g):
    bsz, seq, d = x.shape
    depth = w_ada.shape[0]
    n_heads = b_f.shape[1]
    da = n_heads * HEAD_DIM
    dp = d - da
    n_exp = w_router.shape[2]
    tokens = bsz * seq
    n_assign = tokens * TOP_K
    n_blocks = -(-n_assign // EXPERT_BLOCK) + n_exp
    n_rows = n_blocks * EXPERT_BLOCK

    mod = _ada(c, w_ada, b_ada)
    mod = mod.reshape(depth, bsz, N_MOD, 1, d)

    for l in range(depth):
        sh1, sc1, g1, sh2, sc2, g2 = (mod[l, :, j] for j in range(N_MOD))
        wl = w_in[l]
        wq = wl[:, :da].astype(BF16)
        wk = wl[:, da:2 * da].astype(BF16)
        wv = wl[:, 2 * da:3 * da].astype(BF16)
        wf = _pad_lanes(wl[:, 3 * da:3 * da + n_heads], LANES).astype(BF16)
        wu = wl[:, 3 * da + n_heads:].astype(BF16)
        bf = _pad_lanes(b_f[l][None, :], LANES)
        q, k, v, dcol, drow, p = _proj(
            x, norm1_g[l][None, :], sh1, sc1, wq, wk, wv, wf, bf, wu,
            w_pool[l].astype(BF16), pool_scale[l][None, :], n_heads=n_heads)
        a = _attn(q, k, v, dcol, drow, n_heads=n_heads)

        wr = _pad_lanes(w_router[l], LANES)
        wr_hi = wr.astype(BF16)
        wr_lo = (wr - wr_hi.astype(F32)).astype(BF16)
        wo = w_o[l].astype(BF16)
        x1, h2, idx_t, wts_t, rank_t, counts = _post(
            a, p, x, g1, norm2_g[l][None, :], sh2, sc2, wo[:da], wo[da:],
            jnp.concatenate([wr_hi, wr_lo], axis=1), wr_hi, router_bias[l][:, None])

        counts = counts[:, 0]
        padded = ((counts + EXPERT_BLOCK - 1) // EXPERT_BLOCK) * EXPERT_BLOCK
        cum = jnp.cumsum(padded)
        pstart = jnp.concatenate([jnp.zeros((1,), I32), cum]).astype(I32)
        nused = (cum[-1:] // EXPERT_BLOCK).astype(I32)
        block_starts = jnp.arange(n_blocks, dtype=I32) * EXPERT_BLOCK
        block_exp = jnp.minimum(
            jnp.sum((cum[None, :] <= block_starts[:, None]).astype(I32), axis=1), n_exp - 1)

        pos_t = _positions(pstart, idx_t, rank_t, n_exp)
        xs, shared = _dispatch(pstart, counts, pos_t, h2, w1_shared[l].astype(BF16),
                               w3_shared[l].astype(BF16), w2_shared[l].astype(BF16),
                               n_rows, d // PAIR)
        y = _experts(block_exp, nused, xs, w1, w3, w2, l)
        x = _combine(pos_t, y, wts_t.T, shared.reshape(bsz, seq, d), x1, g2,
                     final_g[None, :], l == depth - 1)
    return x
```

```python
import functools

import jax
import jax.numpy as jnp
from jax import lax
from jax.experimental import pallas as pl
from jax.experimental.pallas import tpu as pltpu

F32 = jnp.float32
BF16 = jnp.bfloat16
I32 = jnp.int32
U32 = jnp.uint32

EPS = 1e-6
LOG2E = 1.4426950408889634
HEAD_DIM = 128
LANES = 128
POOL_WINDOWS = (2, 4, 8, 16)
POOL_HALO = 16
TOP_K = 8
N_EXPERT_GROUPS = 8
TOPK_GROUPS = 4
ROUTED_SCALE = 2.5
EXPERT_BLOCK = 512
ZERO_CHUNK = 64
ZERO_SHIFT = ZERO_CHUNK.bit_length() - 1
assert EXPERT_BLOCK % ZERO_CHUNK == 0 and ZERO_CHUNK == 1 << ZERO_SHIFT
N_MOD = 6
VMEM_LIMIT = 56 * 1024 * 1024

_NT = (((1,), (1,)), ((), ()))


def _cparams(sem):
    return pltpu.CompilerParams(dimension_semantics=sem, vmem_limit_bytes=VMEM_LIMIT)


PAIR = 2 * LANES


def _pack_pair(lo, hi):
    lo = lax.bitcast_convert_type(lo.astype(BF16).astype(F32), U32)
    hi = lax.bitcast_convert_type(hi.astype(BF16).astype(F32), U32)
    return hi | (lo >> 16)


def _unpack_pair(p):
    lo = lax.bitcast_convert_type(p << 16, F32)
    hi = lax.bitcast_convert_type(p & jnp.uint32(0xFFFF0000), F32)
    return lo, hi


def _store_rows(ref, x, m):
    ns = x.shape[1] // PAIR
    for s in range(ns):
        ref[pl.ds(s, m, stride=ns), :] = _pack_pair(x[:, s * PAIR:s * PAIR + LANES],
                                                    x[:, s * PAIR + LANES:(s + 1) * PAIR])


def _load_chunk_bf16(ref, s, m, ns):
    lo, hi = _unpack_pair(ref[pl.ds(s, m, stride=ns), :])
    return jnp.concatenate([lo.astype(BF16), hi.astype(BF16)], axis=1)


def _resident(shape, index_map):
    return pl.BlockSpec(shape, index_map, pipeline_mode=pl.Buffered(1))


def _ada_kernel(c_ref, w_ref, b_ref, o_ref):
    c = c_ref[...]
    ca = c * jax.nn.sigmoid(c)
    o_ref[0] = jnp.dot(ca, w_ref[0], precision=lax.Precision.HIGHEST,
                       preferred_element_type=F32) + b_ref[0]


def _ada(c, w_ada, b_ada):
    depth, d, n = w_ada.shape
    bsz = c.shape[0]
    rows = 8
    cp = jnp.zeros((rows, d), F32).at[:bsz].set(c)
    tn = 1536 if n % 1536 == 0 else 512
    out = pl.pallas_call(
        _ada_kernel,
        out_shape=jax.ShapeDtypeStruct((depth, rows, n), F32),
        grid=(depth, n // tn),
        in_specs=[pl.BlockSpec((rows, d), lambda l, j: (0, 0)),
                  pl.BlockSpec((1, d, tn), lambda l, j: (l, 0, j)),
                  pl.BlockSpec((1, 1, tn), lambda l, j: (l, 0, j))],
        out_specs=pl.BlockSpec((1, rows, tn), lambda l, j: (l, 0, j)),
        compiler_params=_cparams(("arbitrary", "arbitrary")),
        name="ada",
    )(cp, w_ada, b_ada.reshape(depth, 1, n))
    return out[:, :bsz]


def _proj_kernel(x_ref, g_ref, sh_ref, sc_ref, wq_ref, wk_ref, wv_ref, wf_ref, bf_ref,
                 wu_ref, wp_ref, ps_ref,
                 q_ref, k_ref, v_ref, dcol_ref, drow_ref, p_ref,
                 ubuf, dcarry, *, tm, gc):
    i = pl.program_id(1)
    x = x_ref[0]
    ms = jnp.mean(x * x, axis=-1, keepdims=True)
    y = (x * lax.rsqrt(ms + EPS)) * g_ref[...]
    h = y * (1.0 + sc_ref[0]) + sh_ref[0]
    hb = h.astype(BF16)

    scale = HEAD_DIM ** -0.5 * LOG2E
    q_ref[0] = (jnp.dot(hb, wq_ref[...], preferred_element_type=F32) * scale).astype(BF16)
    k_ref[0] = jnp.dot(hb, wk_ref[...], preferred_element_type=F32).astype(BF16)
    v_ref[0] = jnp.dot(hb, wv_ref[...], preferred_element_type=F32).astype(BF16)

    z = jnp.dot(hb, wf_ref[...], preferred_element_type=F32) + bf_ref[...]
    lf = (jnp.minimum(z, 0.0) - jnp.log1p(jnp.exp(-jnp.abs(z)))) * LOG2E

    @pl.when(i == 0)
    def _():
        dcarry[...] = jnp.zeros_like(dcarry)
        ubuf[0:POOL_HALO, :] = jnp.zeros((POOL_HALO, ubuf.shape[1]), F32)

    @pl.when(i > 0)
    def _():
        ubuf[0:POOL_HALO, :] = ubuf[tm:tm + POOL_HALO, :]

    row = lax.broadcasted_iota(I32, (tm, LANES), 0)
    dc = lf
    span = 1
    while span < tm:
        dc = dc + jnp.where(row >= span, pltpu.roll(dc, span, 0), 0.0)
        span *= 2
    dc = dc + dcarry[...]
    dcarry[...] = dc[tm - 1:tm, :]
    dcol_ref[0] = dc
    drow_ref[0] = dc.T[0:drow_ref.shape[1], :]

    u = jnp.dot(hb, wu_ref[...], preferred_element_type=F32)
    ubuf[POOL_HALO:POOL_HALO + tm, :] = u
    pos = i * tm + lax.broadcasted_iota(I32, (tm, 1), 0)
    for g, w in enumerate(POOL_WINDOWS):
        s = ubuf[:, g * gc:(g + 1) * gc]
        step = 1
        while step < w:
            s = s + pltpu.roll(s, step, 0)
            step *= 2
        cnt = jnp.minimum(pos + 1, w).astype(F32)
        pooled = s[POOL_HALO:POOL_HALO + tm, :] / cnt - u[:, g * gc:(g + 1) * gc]
        yg = jnp.dot(pooled.astype(BF16), wp_ref[g], preferred_element_type=F32)
        p_ref[0, :, g * gc:(g + 1) * gc] = (yg * ps_ref[:, g * gc:(g + 1) * gc]).astype(BF16)


def _proj(x, g, sh, sc, wq, wk, wv, wf, bf, wu, wp, ps):
    bsz, seq, d = x.shape
    da = wq.shape[1]
    dp = wu.shape[1]
    gc = dp // len(POOL_WINDOWS)
    tm = min(512, seq)
    nt = seq // tm
    kern = functools.partial(_proj_kernel, tm=tm, gc=gc)
    row = lambda b, i: (b, i, 0)
    per_b = lambda b, i: (b, 0, 0)
    const2 = lambda b, i: (0, 0)
    const3 = lambda b, i: (0, 0, 0)
    return pl.pallas_call(
        kern,
        out_shape=(jax.ShapeDtypeStruct((bsz, seq, da), BF16),
                   jax.ShapeDtypeStruct((bsz, seq, da), BF16),
                   jax.ShapeDtypeStruct((bsz, seq, da), BF16),
                   jax.ShapeDtypeStruct((bsz, seq, LANES), F32),
                   jax.ShapeDtypeStruct((bsz, 8, seq), F32),
                   jax.ShapeDtypeStruct((bsz, seq, dp), BF16)),
        grid=(bsz, nt),
        in_specs=[pl.BlockSpec((1, tm, d), row),
                  pl.BlockSpec((1, d), const2),
                  pl.BlockSpec((1, 1, d), per_b),
                  pl.BlockSpec((1, 1, d), per_b),
                  _resident((d, da), const2),
                  _resident((d, da), const2),
                  _resident((d, da), const2),
                  _resident((d, LANES), const2),
                  pl.BlockSpec((1, LANES), const2),
                  _resident((d, dp), const2),
                  _resident(wp.shape, const3),
                  pl.BlockSpec((1, dp), const2)],
        out_specs=(pl.BlockSpec((1, tm, da), row),
                   pl.BlockSpec((1, tm, da), row),
                   pl.BlockSpec((1, tm, da), row),
                   pl.BlockSpec((1, tm, LANES), row),
                   pl.BlockSpec((1, 8, tm), lambda b, i: (b, 0, i)),
                   pl.BlockSpec((1, tm, dp), row)),
        scratch_shapes=[pltpu.VMEM((tm + POOL_HALO, dp), F32),
                        pltpu.VMEM((1, LANES), F32)],
        compiler_params=_cparams(("arbitrary", "arbitrary")),
        name="proj",
    )(x, g, sh, sc, wq, wk, wv, wf, bf, wu, wp, ps)


def _attn_kernel(q_ref, k_ref, v_ref, dcol_ref, drow_ref, o_ref, *, tq, tk, hp):
    hg = pl.program_id(1)
    i = pl.program_id(2)
    lane = lax.broadcasted_iota(I32, (tq, LANES), 1)
    dcol = dcol_ref[0]
    heads = [slice(u * HEAD_DIM, (u + 1) * HEAD_DIM) for u in range(hp)]
    qs = [q_ref[0, :, hs] for hs in heads]
    dqs = [jnp.sum(jnp.where(lane == hg * hp + u, dcol, 0.0), axis=-1, keepdims=True)
           for u in range(hp)]

    def block(j, carry, diag):
        start = pl.multiple_of(j * tk, tk)
        out = []
        for u in range(hp):
            m, l, acc = carry[u]
            ks = k_ref[0, pl.ds(start, tk), heads[u]]
            vs = v_ref[0, pl.ds(start, tk), heads[u]]
            dk = drow_ref[0, pl.ds(hg * hp + u, 1), pl.ds(start, tk)]
            z = lax.dot_general(qs[u], ks, _NT, preferred_element_type=F32) - dk
            if diag is not None:
                rr = lax.broadcasted_iota(I32, (tq, tk), 0)
                cc = lax.broadcasted_iota(I32, (tq, tk), 1)
                z = jnp.where(cc + diag <= rr, z, -jnp.inf)
            m_new = jnp.maximum(m, jnp.max(z, axis=-1, keepdims=True) + dqs[u])
            p = jnp.exp2(z - (m_new - dqs[u]))
            alpha = jnp.exp2(m - m_new)
            l = alpha * l + jnp.sum(p, axis=-1, keepdims=True)
            acc = alpha * acc + jnp.dot(p.astype(BF16), vs, preferred_element_type=F32)
            out.append((m_new, l, acc))
        return tuple(out)

    init = tuple((jnp.full((tq, 1), -jnp.inf, F32), jnp.zeros((tq, 1), F32),
                  jnp.zeros((tq, HEAD_DIM), F32)) for _ in range(hp))
    nin = tq // tk
    carry = lax.fori_loop(0, i * nin, functools.partial(block, diag=None), init)
    for jj in range(nin):
        carry = block(i * nin + jj, carry, jj * tk)
    for u in range(hp):
        _, l, acc = carry[u]
        o_ref[0, :, heads[u]] = (acc / l).astype(o_ref.dtype)


def _attn(q, k, v, dcol, drow, *, n_heads):
    bsz, seq, da = q.shape
    tq = min(1024, seq)
    tk = tq
    nq = seq // tq
    hp = 2 if n_heads % 2 == 0 else 1
    kern = functools.partial(_attn_kernel, tq=tq, tk=tk, hp=hp)
    return pl.pallas_call(
        kern,
        out_shape=jax.ShapeDtypeStruct((bsz, seq, da), BF16),
        grid=(bsz, n_heads // hp, nq),
        in_specs=[pl.BlockSpec((1, tq, hp * HEAD_DIM), lambda b, h, i: (b, i, h)),
                  pl.BlockSpec((1, seq, hp * HEAD_DIM), lambda b, h, i: (b, 0, h)),
                  pl.BlockSpec((1, seq, hp * HEAD_DIM), lambda b, h, i: (b, 0, h)),
                  pl.BlockSpec((1, tq, LANES), lambda b, h, i: (b, i, 0)),
                  pl.BlockSpec((1, 8, seq), lambda b, h, i: (b, 0, 0))],
        out_specs=pl.BlockSpec((1, tq, hp * HEAD_DIM), lambda b, h, i: (b, i, h)),
        compiler_params=_cparams(("arbitrary", "arbitrary", "arbitrary")),
        name="attn",
    )(q, k, v, dcol, drow)


def _post_kernel(a_ref, p_ref, x_ref, g1_ref, ng_ref, sh_ref, sc_ref, woa_ref, wop_ref,
                 wr1_ref, wr2_ref, rb_ref,
                 x1_ref, h2_ref, idx_ref, wts_ref, rank_ref, cnt_ref,
                 carry, *, tm, n_exp):
    first = jnp.logical_and(pl.program_id(0) == 0, pl.program_id(1) == 0)

    @pl.when(first)
    def _():
        carry[...] = jnp.zeros_like(carry)

    mix = jnp.dot(a_ref[0], woa_ref[...], preferred_element_type=F32)
    mix = mix + jnp.dot(p_ref[0], wop_ref[...], preferred_element_type=F32)
    x1 = x_ref[0] + g1_ref[0] * mix
    x1_ref[0] = x1
    ms = jnp.mean(x1 * x1, axis=-1, keepdims=True)
    y = (x1 * lax.rsqrt(ms + EPS)) * ng_ref[...]
    h2 = y * (1.0 + sc_ref[0]) + sh_ref[0]
    _store_rows(h2_ref, h2, tm)

    h_hi = h2.astype(BF16)
    h_lo = (h2 - h_hi.astype(F32)).astype(BF16)
    a1 = jnp.dot(h_hi, wr1_ref[...], preferred_element_type=F32)
    a2 = jnp.dot(h_lo, wr2_ref[...], preferred_element_type=F32)
    logits = a1[:, :LANES] + a1[:, LANES:] + a2
    lt = logits.T[0:n_exp, :]
    sc = jax.nn.sigmoid(lt)
    biased = sc + rb_ref[...]

    epg = n_exp // N_EXPERT_GROUPS
    sub = lax.broadcasted_iota(I32, (epg, tm), 0)
    blocks, gscore = [], []
    for g in range(N_EXPERT_GROUPS):
        blk = biased[g * epg:(g + 1) * epg, :]
        m1 = jnp.max(blk, axis=0, keepdims=True)
        f1 = jnp.min(jnp.where(blk == m1, sub, epg), axis=0, keepdims=True)
        m2 = jnp.max(jnp.where(sub == f1, -jnp.inf, blk), axis=0, keepdims=True)
        blocks.append(blk)
        gscore.append(m1 + m2)
    masked = []
    for g in range(N_EXPERT_GROUPS):
        beats = jnp.zeros((1, tm), I32)
        for g2 in range(N_EXPERT_GROUPS):
            if g2 == g:
                continue
            win = (gscore[g2] > gscore[g]) if g2 > g else (gscore[g2] >= gscore[g])
            beats = beats + win.astype(I32)
        masked.append(jnp.where(beats < TOPK_GROUPS, blocks[g], -jnp.inf))
    v = jnp.concatenate(masked, axis=0)

    eio = lax.broadcasted_iota(I32, (n_exp, tm), 0)
    base = carry[...]
    hits, svals = [], []
    sel = jnp.zeros((n_exp, tm), F32)
    for k in range(TOP_K):
        m = jnp.max(v, axis=0, keepdims=True)
        idx = jnp.min(jnp.where(v == m, eio, n_exp), axis=0, keepdims=True)
        hit = eio == idx
        svals.append(jnp.sum(jnp.where(hit, sc, 0.0), axis=0, keepdims=True))
        v = jnp.where(hit, -jnp.inf, v)
        sel = sel + hit.astype(F32)
        hits.append(hit)
        idx_ref[k:k + 1, :] = idx
    total = svals[0]
    for k in range(1, TOP_K):
        total = total + svals[k]
    for k in range(TOP_K):
        wts_ref[k:k + 1, :] = svals[k] / total * ROUTED_SCALE

    rr = lax.broadcasted_iota(I32, (tm, tm), 0)
    cc = lax.broadcasted_iota(I32, (tm, tm), 1)
    upper = (rr < cc).astype(BF16)
    before = jnp.dot(sel.astype(BF16), upper, preferred_element_type=F32).astype(I32) + base
    for k in range(TOP_K):
        rank_ref[k:k + 1, :] = jnp.sum(jnp.where(hits[k], before, 0), axis=0, keepdims=True)
    new = base + jnp.sum(sel, axis=1, keepdims=True).astype(I32)
    carry[...] = new
    cnt_ref[...] = new


def _post(a, p, x, g1, ng, sh, sc, woa, wop, wr1, wr2, rb):
    bsz, seq, d = x.shape
    da = a.shape[2]
    dp = p.shape[2]
    n_exp = rb.shape[0]
    tm = min(512, seq)
    nt = seq // tm
    tokens = bsz * seq
    kern = functools.partial(_post_kernel, tm=tm, n_exp=n_exp)
    row = lambda b, i: (b, i, 0)
    per_b = lambda b, i: (b, 0, 0)
    const2 = lambda b, i: (0, 0)
    tok = lambda b, i: (0, b * nt + i)
    return pl.pallas_call(
        kern,
        out_shape=(jax.ShapeDtypeStruct((bsz, seq, d), F32),
                   jax.ShapeDtypeStruct((tokens * (d // PAIR), LANES), U32),
                   jax.ShapeDtypeStruct((TOP_K, tokens), I32),
                   jax.ShapeDtypeStruct((TOP_K, tokens), F32),
                   jax.ShapeDtypeStruct((TOP_K, tokens), I32),
                   jax.ShapeDtypeStruct((n_exp, 1), I32)),
        grid=(bsz, nt),
        in_specs=[pl.BlockSpec((1, tm, da), row),
                  pl.BlockSpec((1, tm, dp), row),
                  pl.BlockSpec((1, tm, d), row),
                  pl.BlockSpec((1, 1, d), per_b),
                  pl.BlockSpec((1, d), const2),
                  pl.BlockSpec((1, 1, d), per_b),
                  pl.BlockSpec((1, 1, d), per_b),
                  _resident((da, d), const2),
                  _resident((dp, d), const2),
                  _resident((d, 2 * LANES), const2),
                  _resident((d, LANES), const2),
                  pl.BlockSpec((n_exp, 1), const2)],
        out_specs=(pl.BlockSpec((1, tm, d), row),
                   pl.BlockSpec((tm * (d // PAIR), LANES), lambda b, i: (b * nt + i, 0)),
                   pl.BlockSpec((TOP_K, tm), tok),
                   pl.BlockSpec((TOP_K, tm), tok),
                   pl.BlockSpec((TOP_K, tm), tok),
                   pl.BlockSpec((n_exp, 1), const2)),
        scratch_shapes=[pltpu.VMEM((n_exp, 1), I32)],
        compiler_params=_cparams(("arbitrary", "arbitrary")),
        name="post",
    )(a, p, x, g1, ng, sh, sc, woa, wop, wr1, wr2, rb)


def _row_tile(ref, t, ns):
    start = t * ns if isinstance(t, int) else pl.multiple_of(t * ns, ns)
    return ref.at[pl.ds(start, ns)]


def _row_copy(src, s, dst, t, sem, ns):
    return pltpu.make_async_copy(_row_tile(src, s, ns), _row_tile(dst, t, ns), sem)


def _positions_kernel(pstart_ref, idx_ref, rank_ref, pos_ref, *, n_exp):
    idx = idx_ref[...]
    pos = rank_ref[...]
    for e in range(n_exp):
        pos = pos + jnp.where(idx == e, pstart_ref[e], 0)
    pos_ref[...] = pos


def _positions(pstart, idx_t, rank_t, n_exp):
    tokens = idx_t.shape[1]
    tp = min(4096, tokens)
    blk = pl.BlockSpec((TOP_K, tp), lambda i, *_: (0, i))
    return pl.pallas_call(
        functools.partial(_positions_kernel, n_exp=n_exp),
        out_shape=jax.ShapeDtypeStruct((TOP_K, tokens), I32),
        grid_spec=pltpu.PrefetchScalarGridSpec(
            num_scalar_prefetch=1, grid=(tokens // tp,), in_specs=[blk, blk], out_specs=blk),
        compiler_params=_cparams(("arbitrary",)),
        name="positions",
    )(pstart, idx_t, rank_t)


def _dispatch_kernel(pstart_ref, cnt_ref, pos_ref, h_ref, w1_ref, w3_ref, w2_ref,
                     xs_ref, sh_ref, zrow, sem, zsem, *, td, n_exp, ns):
    step = pl.program_id(0)

    @pl.when(step == 0)
    def _():
        zrow[...] = jnp.zeros_like(zrow)
        chunk_rows = ZERO_CHUNK * ns

        def chunk_copy(first_row):
            start = (first_row * ns if isinstance(first_row, int)
                     else pl.multiple_of(first_row * ns, chunk_rows))
            return pltpu.make_async_copy(zrow, xs_ref.at[pl.ds(start, chunk_rows)], zsem)

        def per_expert(e, carry):
            n_single, n_chunk = carry
            lo = pstart_ref[e] + cnt_ref[e]
            hi = pstart_ref[e + 1]
            mid = ((lo + (ZERO_CHUNK - 1)) >> ZERO_SHIFT) << ZERO_SHIFT

            def fill_row(r, c):
                _row_copy(zrow, 0, xs_ref, r, zsem, ns).start()
                return c

            def fill_chunk(j, c):
                chunk_copy(mid + j * ZERO_CHUNK).start()
                return c

            chunks = (hi - mid) >> ZERO_SHIFT
            lax.fori_loop(lo, mid, fill_row, 0)
            lax.fori_loop(0, chunks, fill_chunk, 0)
            return n_single + (mid - lo), n_chunk + chunks

        n_single, n_chunk = lax.fori_loop(0, n_exp, per_expert, (jnp.int32(0), jnp.int32(0)))

        def drain_row(r, c):
            _row_copy(zrow, 0, xs_ref, 0, zsem, ns).wait()
            return c

        def drain_chunk(j, c):
            chunk_copy(0).wait()
            return c

        lax.fori_loop(0, n_single, drain_row, 0)
        lax.fori_loop(0, n_chunk, drain_chunk, 0)

    for t in range(td):
        for k in range(TOP_K):
            _row_copy(h_ref, t, xs_ref, pos_ref[k, t], sem, ns).start(priority=k % 2)

    hk = _load_chunk_bf16(h_ref, 0, td, ns)
    h1 = jnp.dot(hk, w1_ref[0:PAIR, :], preferred_element_type=F32)
    h3 = jnp.dot(hk, w3_ref[0:PAIR, :], preferred_element_type=F32)
    for s in range(1, ns):
        hk = _load_chunk_bf16(h_ref, s, td, ns)
        h1 = h1 + jnp.dot(hk, w1_ref[s * PAIR:(s + 1) * PAIR, :], preferred_element_type=F32)
        h3 = h3 + jnp.dot(hk, w3_ref[s * PAIR:(s + 1) * PAIR, :], preferred_element_type=F32)
    act = (h1 * jax.nn.sigmoid(h1) * h3).astype(BF16)
    sh_ref[...] = jnp.dot(act, w2_ref[...], preferred_element_type=F32)

    for k in range(TOP_K):
        pltpu.make_async_copy(h_ref, xs_ref.at[pl.ds(0, td * ns)], sem).wait()


def _dispatch(pstart, counts, pos_t, h2, w1s, w3s, w2s, n_rows, ns):
    tokens = h2.shape[0] // ns
    n_exp = counts.shape[0]
    d, de = w1s.shape
    td = min(512, tokens)
    kern = functools.partial(_dispatch_kernel, td=td, n_exp=n_exp, ns=ns)
    const2 = lambda i, *_: (0, 0)
    return pl.pallas_call(
        kern,
        out_shape=(jax.ShapeDtypeStruct((n_rows * ns, LANES), U32),
                   jax.ShapeDtypeStruct((tokens, d), F32)),
        grid_spec=pltpu.PrefetchScalarGridSpec(
            num_scalar_prefetch=2,
            grid=(tokens // td,),
            in_specs=[pl.BlockSpec((TOP_K, td), lambda i, *_: (0, i), memory_space=pltpu.SMEM),
                      pl.BlockSpec((td * ns, LANES), lambda i, *_: (i, 0)),
                      _resident((d, de), const2),
                      _resident((d, de), const2),
                      _resident((de, d), const2)],
            out_specs=(pl.BlockSpec(memory_space=pl.ANY),
                       pl.BlockSpec((td, d), lambda i, *_: (i, 0))),
            scratch_shapes=[pltpu.VMEM((ZERO_CHUNK * ns, LANES), U32),
                            pltpu.SemaphoreType.DMA, pltpu.SemaphoreType.DMA]),
        compiler_params=_cparams(("arbitrary",)),
        name="dispatch",
    )(pstart, counts, pos_t, h2, w1s, w3s, w2s)


def _experts_kernel(bexp_ref, nused_ref, x_ref, w1_ref, w3_ref, w2_ref, y_ref,
                    wb1, wb3, wb2, xb):
    b = pl.program_id(0)
    live = b < nused_ref[0]
    changed = jnp.logical_or(b == 0, bexp_ref[b] != bexp_ref[jnp.maximum(b - 1, 0)])

    @pl.when(jnp.logical_and(live, changed))
    def _():
        wb1[...] = w1_ref[0, 0].astype(BF16)
        wb3[...] = w3_ref[0, 0].astype(BF16)
        wb2[...] = w2_ref[0, 0].astype(BF16)

    @pl.when(live)
    def _():
        ns = wb1.shape[0] // PAIR
        for s in range(ns):
            xb[:, s * PAIR:(s + 1) * PAIR] = _load_chunk_bf16(x_ref, s, EXPERT_BLOCK, ns)
        h1 = jnp.dot(xb[...], wb1[...], preferred_element_type=F32)
        h3 = jnp.dot(xb[...], wb3[...], preferred_element_type=F32)
        act = (h1 * jax.nn.sigmoid(h1) * h3).astype(BF16)
        _store_rows(y_ref, jnp.dot(act, wb2[...], preferred_element_type=F32), EXPERT_BLOCK)


def _experts(block_exp, nused, xs, w1, w3, w2, layer):
    d, de = w1.shape[2], w1.shape[3]
    ns = d // PAIR
    nb = xs.shape[0] // (EXPERT_BLOCK * ns)
    blk = lambda b, be, nu: (jnp.minimum(b, nu[0] - 1), 0)
    wsel = lambda b, be, nu: (layer, be[jnp.minimum(b, nu[0] - 1)], 0, 0)
    return pl.pallas_call(
        _experts_kernel,
        out_shape=jax.ShapeDtypeStruct(xs.shape, U32),
        grid_spec=pltpu.PrefetchScalarGridSpec(
            num_scalar_prefetch=2,
            grid=(nb,),
            in_specs=[pl.BlockSpec((EXPERT_BLOCK * ns, LANES), blk),
                      pl.BlockSpec((1, 1, d, de), wsel),
                      pl.BlockSpec((1, 1, d, de), wsel),
                      pl.BlockSpec((1, 1, de, d), wsel)],
            out_specs=pl.BlockSpec((EXPERT_BLOCK * ns, LANES), blk),
            scratch_shapes=[pltpu.VMEM((d, de), BF16), pltpu.VMEM((d, de), BF16),
                            pltpu.VMEM((de, d), BF16), pltpu.VMEM((EXPERT_BLOCK, d), BF16)]),
        compiler_params=_cparams(("arbitrary",)),
        name="experts",
    )(block_exp, nused, xs, w1, w3, w2)


def _combine_kernel(pos_ref, y_ref, wts_ref, sh_ref, x1_ref, g2_ref, fg_ref, o_ref, buf, sem,
                    *, tc, ns, last_layer):
    for t in range(tc):
        for k in range(TOP_K):
            pltpu.make_async_copy(_row_tile(y_ref, pos_ref[k, t], ns),
                                  _row_tile(buf.at[k], t, ns), sem).start(priority=k % 2)

    for k in range(TOP_K):
        pltpu.make_async_copy(y_ref.at[pl.ds(0, tc * ns)], buf.at[k], sem).wait()

    w = wts_ref[...]
    g2 = g2_ref[0]
    for s in range(ns):
        r_lo, r_hi = _unpack_pair(buf.at[0][pl.ds(s, tc, stride=ns), :])
        r_lo = r_lo * w[:, 0:1]
        r_hi = r_hi * w[:, 0:1]
        for k in range(1, TOP_K):
            y_lo, y_hi = _unpack_pair(buf.at[k][pl.ds(s, tc, stride=ns), :])
            r_lo = r_lo + y_lo * w[:, k:k + 1]
            r_hi = r_hi + y_hi * w[:, k:k + 1]
        c0 = slice(s * PAIR, s * PAIR + LANES)
        c1 = slice(s * PAIR + LANES, (s + 1) * PAIR)
        o_ref[0, :, c0] = x1_ref[0, :, c0] + g2[:, c0] * (r_lo + sh_ref[0, :, c0])
        o_ref[0, :, c1] = x1_ref[0, :, c1] + g2[:, c1] * (r_hi + sh_ref[0, :, c1])
    if last_layer:
        x = o_ref[0]
        ms = jnp.mean(x * x, axis=-1, keepdims=True)
        o_ref[0] = (x * lax.rsqrt(ms + EPS)) * fg_ref[...]


def _combine(pos_t, y, wts_tok, shared, x1, g2, final_g, last_layer):
    bsz, seq, d = x1.shape
    ns = d // PAIR
    tc = min(512, seq)
    nt = seq // tc
    kern = functools.partial(_combine_kernel, tc=tc, ns=ns, last_layer=last_layer)
    row = lambda b, i: (b, i, 0)
    return pl.pallas_call(
        kern,
        out_shape=jax.ShapeDtypeStruct((bsz, seq, d), F32),
        grid=(bsz, nt),
        in_specs=[pl.BlockSpec((TOP_K, tc), lambda b, i: (0, b * nt + i),
                               memory_space=pltpu.SMEM),
                  pl.BlockSpec(memory_space=pl.ANY),
                  pl.BlockSpec((tc, TOP_K), lambda b, i: (b * nt + i, 0)),
                  pl.BlockSpec((1, tc, d), row),
                  pl.BlockSpec((1, tc, d), row),
                  pl.BlockSpec((1, 1, d), lambda b, i: (b, 0, 0)),
                  pl.BlockSpec((1, d), lambda b, i: (0, 0))],
        out_specs=pl.BlockSpec((1, tc, d), row),
        scratch_shapes=[pltpu.VMEM((TOP_K, tc * ns, LANES), U32), pltpu.SemaphoreType.DMA],
        compiler_params=_cparams(("arbitrary", "arbitrary")),
        name="combine",
    )(pos_t, y, wts_tok, shared, x1, g2, final_g)


def _pad_lanes(w, n):
    return jnp.pad(w, ((0, 0), (0, n - w.shape[1])))


def kernel(x, c, norm1_g, norm2_g, w_ada, b_ada, w_in, b_f, w_pool, pool_scale, w_o,
           w_router, router_bias, w1, w3, w2, w1_shared, w3_shared, w2_shared, final_g):
    bsz, seq, d = x.shape
    depth = w_ada.shape[0]
    n_heads = b_f.shape[1]
    da = n_heads * HEAD_DIM
    dp = d - da
    n_exp = w_router.shape[2]
    tokens = bsz * seq
    n_assign = tokens * TOP_K
    n_blocks = -(-n_assign // EXPERT_BLOCK) + n_exp
    n_rows = n_blocks * EXPERT_BLOCK

    mod = _ada(c, w_ada, b_ada)
    mod = mod.reshape(depth, bsz, N_MOD, 1, d)

    for l in range(depth):
        sh1, sc1, g1, sh2, sc2, g2 = (mod[l, :, j] for j in range(N_MOD))
        wl = w_in[l]
        wq = wl[:, :da].astype(BF16)
        wk = wl[:, da:2 * da].astype(BF16)
        wv = wl[:, 2 * da:3 * da].astype(BF16)
        wf = _pad_lanes(wl[:, 3 * da:3 * da + n_heads], LANES).astype(BF16)
        wu = wl[:, 3 * da + n_heads:].astype(BF16)
        bf = _pad_lanes(b_f[l][None, :], LANES)
        q, k, v, dcol, drow, p = _proj(
            x, norm1_g[l][None, :], sh1, sc1, wq, wk, wv, wf, bf, wu,
            w_pool[l].astype(BF16), pool_scale[l][None, :])
        a = _attn(q, k, v, dcol, drow, n_heads=n_heads)

        wr = _pad_lanes(w_router[l], LANES)
        wr_hi = wr.astype(BF16)
        wr_lo = (wr - wr_hi.astype(F32)).astype(BF16)
        wo = w_o[l].astype(BF16)
        x1, h2, idx_t, wts_t, rank_t, counts = _post(
            a, p, x, g1, norm2_g[l][None, :], sh2, sc2, wo[:da], wo[da:],
            jnp.concatenate([wr_hi, wr_lo], axis=1), wr_hi, router_bias[l][:, None])

        counts = counts[:, 0]
        padded = ((counts + EXPERT_BLOCK - 1) // EXPERT_BLOCK) * EXPERT_BLOCK
        cum = jnp.cumsum(padded)
        pstart = jnp.concatenate([jnp.zeros((1,), I32), cum]).astype(I32)
        nused = (cum[-1:] // EXPERT_BLOCK).astype(I32)
        block_starts = jnp.arange(n_blocks, dtype=I32) * EXPERT_BLOCK
        block_exp = jnp.minimum(
            jnp.sum((cum[None, :] <= block_starts[:, None]).astype(I32), axis=1), n_exp - 1)

        pos_t = _positions(pstart, idx_t, rank_t, n_exp)
        xs, shared = _dispatch(pstart, counts, pos_t, h2, w1_shared[l].astype(BF16),
                               w3_shared[l].astype(BF16), w2_shared[l].astype(BF16),
                               n_rows, d // PAIR)
        y = _experts(block_exp, nused, xs, w1, w3, w2, l)
        x = _combine(pos_t, y, wts_t.T, shared.reshape(bsz, seq, d), x1, g2,
                     final_g[None, :], l == depth - 1)
    return x
```

```python
import functools

import jax
import jax.numpy as jnp
from jax import lax
from jax.experimental import pallas as pl
from jax.experimental.pallas import tpu as pltpu

F32 = jnp.float32
BF16 = jnp.bfloat16
I32 = jnp.int32
U32 = jnp.uint32

EPS = 1e-6
LOG2E = 1.4426950408889634
HEAD_DIM = 128
LANES = 128
POOL_WINDOWS = (2, 4, 8, 16)
POOL_HALO = 16
TOP_K = 8
N_EXPERT_GROUPS = 8
TOPK_GROUPS = 4
ROUTED_SCALE = 2.5
EXPERT_BLOCK = 512
COMBINE_PARTS = 4
ZERO_CHUNK = 64
ZERO_SHIFT = ZERO_CHUNK.bit_length() - 1
assert EXPERT_BLOCK % ZERO_CHUNK == 0 and ZERO_CHUNK == 1 << ZERO_SHIFT
N_MOD = 6
VMEM_LIMIT = 56 * 1024 * 1024

_NT = (((1,), (1,)), ((), ()))


def _cparams(sem):
    return pltpu.CompilerParams(dimension_semantics=sem, vmem_limit_bytes=VMEM_LIMIT)


PAIR = 2 * LANES


def _pack_pair(lo, hi):
    lo = lax.bitcast_convert_type(lo.astype(BF16).astype(F32), U32)
    hi = lax.bitcast_convert_type(hi.astype(BF16).astype(F32), U32)
    return hi | (lo >> 16)


def _unpack_pair(p):
    lo = lax.bitcast_convert_type(p << 16, F32)
    hi = lax.bitcast_convert_type(p & jnp.uint32(0xFFFF0000), F32)
    return lo, hi


def _store_rows(ref, x, m):
    ns = x.shape[1] // PAIR
    for s in range(ns):
        ref[pl.ds(s, m, stride=ns), :] = _pack_pair(x[:, s * PAIR:s * PAIR + LANES],
                                                    x[:, s * PAIR + LANES:(s + 1) * PAIR])


def _load_chunk_bf16(ref, s, m, ns):
    lo, hi = _unpack_pair(ref[pl.ds(s, m, stride=ns), :])
    return jnp.concatenate([lo.astype(BF16), hi.astype(BF16)], axis=1)


def _resident(shape, index_map):
    return pl.BlockSpec(shape, index_map, pipeline_mode=pl.Buffered(1))


def _ada_kernel(c_ref, w_ref, b_ref, o_ref):
    c = c_ref[...]
    ca = c * jax.nn.sigmoid(c)
    o_ref[0] = jnp.dot(ca, w_ref[0], precision=lax.Precision.HIGHEST,
                       preferred_element_type=F32) + b_ref[0]


def _ada(c, w_ada, b_ada):
    depth, d, n = w_ada.shape
    bsz = c.shape[0]
    rows = 8
    cp = jnp.zeros((rows, d), F32).at[:bsz].set(c)
    tn = 1536 if n % 1536 == 0 else 512
    out = pl.pallas_call(
        _ada_kernel,
        out_shape=jax.ShapeDtypeStruct((depth, rows, n), F32),
        grid=(depth, n // tn),
        in_specs=[pl.BlockSpec((rows, d), lambda l, j: (0, 0)),
                  pl.BlockSpec((1, d, tn), lambda l, j: (l, 0, j)),
                  pl.BlockSpec((1, 1, tn), lambda l, j: (l, 0, j))],
        out_specs=pl.BlockSpec((1, rows, tn), lambda l, j: (l, 0, j)),
        compiler_params=_cparams(("arbitrary", "arbitrary")),
        name="ada",
    )(cp, w_ada, b_ada.reshape(depth, 1, n))
    return out[:, :bsz]


def _proj_kernel(x_ref, g_ref, sh_ref, sc_ref, wq_ref, wk_ref, wv_ref, wf_ref, bf_ref,
                 wu_ref, wp_ref, ps_ref,
                 q_ref, k_ref, v_ref, dcol_ref, drow_ref, p_ref,
                 ubuf, dcarry, *, tm, gc):
    i = pl.program_id(1)
    x = x_ref[0]
    ms = jnp.mean(x * x, axis=-1, keepdims=True)
    y = (x * lax.rsqrt(ms + EPS)) * g_ref[...]
    h = y * (1.0 + sc_ref[0]) + sh_ref[0]
    hb = h.astype(BF16)

    scale = HEAD_DIM ** -0.5 * LOG2E
    q_ref[0] = (jnp.dot(hb, wq_ref[...], preferred_element_type=F32) * scale).astype(BF16)
    k_ref[0] = jnp.dot(hb, wk_ref[...], preferred_element_type=F32).astype(BF16)
    v_ref[0] = jnp.dot(hb, wv_ref[...], preferred_element_type=F32).astype(BF16)

    z = jnp.dot(hb, wf_ref[...], preferred_element_type=F32) + bf_ref[...]
    lf = (jnp.minimum(z, 0.0) - jnp.log1p(jnp.exp(-jnp.abs(z)))) * LOG2E

    @pl.when(i == 0)
    def _():
        dcarry[...] = jnp.zeros_like(dcarry)
        ubuf[0:POOL_HALO, :] = jnp.zeros((POOL_HALO, ubuf.shape[1]), F32)

    @pl.when(i > 0)
    def _():
        ubuf[0:POOL_HALO, :] = ubuf[tm:tm + POOL_HALO, :]

    row = lax.broadcasted_iota(I32, (tm, LANES), 0)
    dc = lf
    span = 1
    while span < tm:
        dc = dc + jnp.where(row >= span, pltpu.roll(dc, span, 0), 0.0)
        span *= 2
    dc = dc + dcarry[...]
    dcarry[...] = dc[tm - 1:tm, :]
    dcol_ref[0] = dc
    drow_ref[0] = dc.T[0:drow_ref.shape[1], :]

    u = jnp.dot(hb, wu_ref[...], preferred_element_type=F32)
    ubuf[POOL_HALO:POOL_HALO + tm, :] = u
    pos = i * tm + lax.broadcasted_iota(I32, (tm, 1), 0)
    for g, w in enumerate(POOL_WINDOWS):
        s = ubuf[:, g * gc:(g + 1) * gc]
        step = 1
        while step < w:
            s = s + pltpu.roll(s, step, 0)
            step *= 2
        cnt = jnp.minimum(pos + 1, w).astype(F32)
        pooled = s[POOL_HALO:POOL_HALO + tm, :] / cnt - u[:, g * gc:(g + 1) * gc]
        yg = jnp.dot(pooled.astype(BF16), wp_ref[g], preferred_element_type=F32)
        p_ref[0, :, g * gc:(g + 1) * gc] = (yg * ps_ref[:, g * gc:(g + 1) * gc]).astype(BF16)


def _proj(x, g, sh, sc, wq, wk, wv, wf, bf, wu, wp, ps):
    bsz, seq, d = x.shape
    da = wq.shape[1]
    dp = wu.shape[1]
    gc = dp // len(POOL_WINDOWS)
    tm = min(512, seq)
    nt = seq // tm
    kern = functools.partial(_proj_kernel, tm=tm, gc=gc)
    row = lambda b, i: (b, i, 0)
    per_b = lambda b, i: (b, 0, 0)
    const2 = lambda b, i: (0, 0)
    const3 = lambda b, i: (0, 0, 0)
    return pl.pallas_call(
        kern,
        out_shape=(jax.ShapeDtypeStruct((bsz, seq, da), BF16),
                   jax.ShapeDtypeStruct((bsz, seq, da), BF16),
                   jax.ShapeDtypeStruct((bsz, seq, da), BF16),
                   jax.ShapeDtypeStruct((bsz, seq, LANES), F32),
                   jax.ShapeDtypeStruct((bsz, 8, seq), F32),
                   jax.ShapeDtypeStruct((bsz, seq, dp), BF16)),
        grid=(bsz, nt),
        in_specs=[pl.BlockSpec((1, tm, d), row),
                  pl.BlockSpec((1, d), const2),
                  pl.BlockSpec((1, 1, d), per_b),
                  pl.BlockSpec((1, 1, d), per_b),
                  _resident((d, da), const2),
                  _resident((d, da), const2),
                  _resident((d, da), const2),
                  _resident((d, LANES), const2),
                  pl.BlockSpec((1, LANES), const2),
                  _resident((d, dp), const2),
                  _resident(wp.shape, const3),
                  pl.BlockSpec((1, dp), const2)],
        out_specs=(pl.BlockSpec((1, tm, da), row),
                   pl.BlockSpec((1, tm, da), row),
                   pl.BlockSpec((1, tm, da), row),
                   pl.BlockSpec((1, tm, LANES), row),
                   pl.BlockSpec((1, 8, tm), lambda b, i: (b, 0, i)),
                   pl.BlockSpec((1, tm, dp), row)),
        scratch_shapes=[pltpu.VMEM((tm + POOL_HALO, dp), F32),
                        pltpu.VMEM((1, LANES), F32)],
        compiler_params=_cparams(("arbitrary", "arbitrary")),
        name="proj",
    )(x, g, sh, sc, wq, wk, wv, wf, bf, wu, wp, ps)


def _attn_kernel(q_ref, k_ref, v_ref, dcol_ref, drow_ref, o_ref, *, tq, tk, hp):
    hg = pl.program_id(1)
    i = pl.program_id(2)
    lane = lax.broadcasted_iota(I32, (tq, LANES), 1)
    dcol = dcol_ref[0]
    heads = [slice(u * HEAD_DIM, (u + 1) * HEAD_DIM) for u in range(hp)]
    qs = [q_ref[0, :, hs] for hs in heads]
    dqs = [jnp.sum(jnp.where(lane == hg * hp + u, dcol, 0.0), axis=-1, keepdims=True)
           for u in range(hp)]

    def block(j, carry, diag):
        start = pl.multiple_of(j * tk, tk)
        out = []
        for u in range(hp):
            m, l, acc = carry[u]
            ks = k_ref[0, pl.ds(start, tk), heads[u]]
            vs = v_ref[0, pl.ds(start, tk), heads[u]]
            dk = drow_ref[0, pl.ds(hg * hp + u, 1), pl.ds(start, tk)]
            z = lax.dot_general(qs[u], ks, _NT, preferred_element_type=F32) - dk
            if diag is not None:
                rr = lax.broadcasted_iota(I32, (tq, tk), 0)
                cc = lax.broadcasted_iota(I32, (tq, tk), 1)
                z = jnp.where(cc + diag <= rr, z, -jnp.inf)
            m_new = jnp.maximum(m, jnp.max(z, axis=-1, keepdims=True) + dqs[u])
            p = jnp.exp2(z - (m_new - dqs[u]))
            alpha = jnp.exp2(m - m_new)
            l = alpha * l + jnp.sum(p, axis=-1, keepdims=True)
            acc = alpha * acc + jnp.dot(p.astype(BF16), vs, preferred_element_type=F32)
            out.append((m_new, l, acc))
        return tuple(out)

    init = tuple((jnp.full((tq, 1), -jnp.inf, F32), jnp.zeros((tq, 1), F32),
                  jnp.zeros((tq, HEAD_DIM), F32)) for _ in range(hp))
    nin = tq // tk
    carry = lax.fori_loop(0, i * nin, functools.partial(block, diag=None), init)
    for jj in range(nin):
        carry = block(i * nin + jj, carry, jj * tk)
    for u in range(hp):
        _, l, acc = carry[u]
        o_ref[0, :, heads[u]] = (acc / l).astype(o_ref.dtype)


def _attn(q, k, v, dcol, drow, *, n_heads):
    bsz, seq, da = q.shape
    tq = min(1024, seq)
    tk = tq
    nq = seq // tq
    hp = 2 if n_heads % 2 == 0 else 1
    kern = functools.partial(_attn_kernel, tq=tq, tk=tk, hp=hp)
    return pl.pallas_call(
        kern,
        out_shape=jax.ShapeDtypeStruct((bsz, seq, da), BF16),
        grid=(bsz, n_heads // hp, nq),
        in_specs=[pl.BlockSpec((1, tq, hp * HEAD_DIM), lambda b, h, i: (b, i, h)),
                  pl.BlockSpec((1, seq, hp * HEAD_DIM), lambda b, h, i: (b, 0, h)),
                  pl.BlockSpec((1, seq, hp * HEAD_DIM), lambda b, h, i: (b, 0, h)),
                  pl.BlockSpec((1, tq, LANES), lambda b, h, i: (b, i, 0)),
                  pl.BlockSpec((1, 8, seq), lambda b, h, i: (b, 0, 0))],
        out_specs=pl.BlockSpec((1, tq, hp * HEAD_DIM), lambda b, h, i: (b, i, h)),
        compiler_params=_cparams(("arbitrary", "arbitrary", "arbitrary")),
        name="attn",
    )(q, k, v, dcol, drow)


def _post_kernel(a_ref, p_ref, x_ref, g1_ref, ng_ref, sh_ref, sc_ref, woa_ref, wop_ref,
                 wr1_ref, wr2_ref, rb_ref,
                 x1_ref, h2_ref, idx_ref, wts_ref, rank_ref, cnt_ref,
                 carry, *, tm, n_exp):
    first = jnp.logical_and(pl.program_id(0) == 0, pl.program_id(1) == 0)

    @pl.when(first)
    def _():
        carry[...] = jnp.zeros_like(carry)

    mix = jnp.dot(a_ref[0], woa_ref[...], preferred_element_type=F32)
    mix = mix + jnp.dot(p_ref[0], wop_ref[...], preferred_element_type=F32)
    x1 = x_ref[0] + g1_ref[0] * mix
    x1_ref[0] = x1
    ms = jnp.mean(x1 * x1, axis=-1, keepdims=True)
    y = (x1 * lax.rsqrt(ms + EPS)) * ng_ref[...]
    h2 = y * (1.0 + sc_ref[0]) + sh_ref[0]
    _store_rows(h2_ref, h2, tm)

    h_hi = h2.astype(BF16)
    h_lo = (h2 - h_hi.astype(F32)).astype(BF16)
    a1 = jnp.dot(h_hi, wr1_ref[...], preferred_element_type=F32)
    a2 = jnp.dot(h_lo, wr2_ref[...], preferred_element_type=F32)
    logits = a1[:, :LANES] + a1[:, LANES:] + a2
    lt = logits.T[0:n_exp, :]
    sc = jax.nn.sigmoid(lt)
    biased = sc + rb_ref[...]

    epg = n_exp // N_EXPERT_GROUPS
    sub = lax.broadcasted_iota(I32, (epg, tm), 0)
    blocks, gscore = [], []
    for g in range(N_EXPERT_GROUPS):
        blk = biased[g * epg:(g + 1) * epg, :]
        m1 = jnp.max(blk, axis=0, keepdims=True)
        f1 = jnp.min(jnp.where(blk == m1, sub, epg), axis=0, keepdims=True)
        m2 = jnp.max(jnp.where(sub == f1, -jnp.inf, blk), axis=0, keepdims=True)
        blocks.append(blk)
        gscore.append(m1 + m2)
    masked = []
    for g in range(N_EXPERT_GROUPS):
        beats = jnp.zeros((1, tm), I32)
        for g2 in range(N_EXPERT_GROUPS):
            if g2 == g:
                continue
            win = (gscore[g2] > gscore[g]) if g2 > g else (gscore[g2] >= gscore[g])
            beats = beats + win.astype(I32)
        masked.append(jnp.where(beats < TOPK_GROUPS, blocks[g], -jnp.inf))
    v = jnp.concatenate(masked, axis=0)

    eio = lax.broadcasted_iota(I32, (n_exp, tm), 0)
    base = carry[...]
    hits, svals = [], []
    sel = jnp.zeros((n_exp, tm), F32)
    for k in range(TOP_K):
        m = jnp.max(v, axis=0, keepdims=True)
        idx = jnp.min(jnp.where(v == m, eio, n_exp), axis=0, keepdims=True)
        hit = eio == idx
        svals.append(jnp.sum(jnp.where(hit, sc, 0.0), axis=0, keepdims=True))
        v = jnp.where(hit, -jnp.inf, v)
        sel = sel + hit.astype(F32)
        hits.append(hit)
        idx_ref[k:k + 1, :] = idx
    total = svals[0]
    for k in range(1, TOP_K):
        total = total + svals[k]
    for k in range(TOP_K):
        wts_ref[k:k + 1, :] = svals[k] / total * ROUTED_SCALE

    rr = lax.broadcasted_iota(I32, (tm, tm), 0)
    cc = lax.broadcasted_iota(I32, (tm, tm), 1)
    upper = (rr < cc).astype(BF16)
    before = jnp.dot(sel.astype(BF16), upper, preferred_element_type=F32).astype(I32) + base
    for k in range(TOP_K):
        rank_ref[k:k + 1, :] = jnp.sum(jnp.where(hits[k], before, 0), axis=0, keepdims=True)
    new = base + jnp.sum(sel, axis=1, keepdims=True).astype(I32)
    carry[...] = new
    cnt_ref[...] = new


def _post(a, p, x, g1, ng, sh, sc, woa, wop, wr1, wr2, rb):
    bsz, seq, d = x.shape
    da = a.shape[2]
    dp = p.shape[2]
    n_exp = rb.shape[0]
    tm = min(512, seq)
    nt = seq // tm
    tokens = bsz * seq
    kern = functools.partial(_post_kernel, tm=tm, n_exp=n_exp)
    row = lambda b, i: (b, i, 0)
    per_b = lambda b, i: (b, 0, 0)
    const2 = lambda b, i: (0, 0)
    tok = lambda b, i: (0, b * nt + i)
    return pl.pallas_call(
        kern,
        out_shape=(jax.ShapeDtypeStruct((bsz, seq, d), F32),
                   jax.ShapeDtypeStruct((tokens * (d // PAIR), LANES), U32),
                   jax.ShapeDtypeStruct((TOP_K, tokens), I32),
                   jax.ShapeDtypeStruct((TOP_K, tokens), F32),
                   jax.ShapeDtypeStruct((TOP_K, tokens), I32),
                   jax.ShapeDtypeStruct((n_exp, 1), I32)),
        grid=(bsz, nt),
        in_specs=[pl.BlockSpec((1, tm, da), row),
                  pl.BlockSpec((1, tm, dp), row),
                  pl.BlockSpec((1, tm, d), row),
                  pl.BlockSpec((1, 1, d), per_b),
                  pl.BlockSpec((1, d), const2),
                  pl.BlockSpec((1, 1, d), per_b),
                  pl.BlockSpec((1, 1, d), per_b),
                  _resident((da, d), const2),
                  _resident((dp, d), const2),
                  _resident((d, 2 * LANES), const2),
                  _resident((d, LANES), const2),
                  pl.BlockSpec((n_exp, 1), const2)],
        out_specs=(pl.BlockSpec((1, tm, d), row),
                   pl.BlockSpec((tm * (d // PAIR), LANES), lambda b, i: (b * nt + i, 0)),
                   pl.BlockSpec((TOP_K, tm), tok),
                   pl.BlockSpec((TOP_K, tm), tok),
                   pl.BlockSpec((TOP_K, tm), tok),
                   pl.BlockSpec((n_exp, 1), const2)),
        scratch_shapes=[pltpu.VMEM((n_exp, 1), I32)],
        compiler_params=_cparams(("arbitrary", "arbitrary")),
        name="post",
    )(a, p, x, g1, ng, sh, sc, woa, wop, wr1, wr2, rb)


def _row_tile(ref, t, ns):
    start = t * ns if isinstance(t, int) else pl.multiple_of(t * ns, ns)
    return ref.at[pl.ds(start, ns)]


def _row_copy(src, s, dst, t, sem, ns):
    return pltpu.make_async_copy(_row_tile(src, s, ns), _row_tile(dst, t, ns), sem)


def _positions_kernel(pstart_ref, idx_ref, rank_ref, pos_ref, *, n_exp):
    idx = idx_ref[...]
    pos = rank_ref[...]
    for e in range(n_exp):
        pos = pos + jnp.where(idx == e, pstart_ref[e], 0)
    pos_ref[...] = pos


def _positions(pstart, idx_t, rank_t, n_exp):
    tokens = idx_t.shape[1]
    tp = min(4096, tokens)
    blk = pl.BlockSpec((TOP_K, tp), lambda i, *_: (0, i))
    return pl.pallas_call(
        functools.partial(_positions_kernel, n_exp=n_exp),
        out_shape=jax.ShapeDtypeStruct((TOP_K, tokens), I32),
        grid_spec=pltpu.PrefetchScalarGridSpec(
            num_scalar_prefetch=1, grid=(tokens // tp,), in_specs=[blk, blk], out_specs=blk),
        compiler_params=_cparams(("arbitrary",)),
        name="positions",
    )(pstart, idx_t, rank_t)


def _dispatch_kernel(pstart_ref, cnt_ref, pos_ref, h_ref, w1_ref, w3_ref, w2_ref,
                     xs_ref, sh_ref, zrow, sem, zsem, *, td, n_exp, ns):
    step = pl.program_id(0)

    @pl.when(step == 0)
    def _():
        zrow[...] = jnp.zeros_like(zrow)
        chunk_rows = ZERO_CHUNK * ns

        def chunk_copy(first_row):
            start = (first_row * ns if isinstance(first_row, int)
                     else pl.multiple_of(first_row * ns, chunk_rows))
            return pltpu.make_async_copy(zrow, xs_ref.at[pl.ds(start, chunk_rows)], zsem)

        def per_expert(e, carry):
            n_single, n_chunk = carry
            lo = pstart_ref[e] + cnt_ref[e]
            hi = pstart_ref[e + 1]
            mid = ((lo + (ZERO_CHUNK - 1)) >> ZERO_SHIFT) << ZERO_SHIFT

            def fill_row(r, c):
                _row_copy(zrow, 0, xs_ref, r, zsem, ns).start()
                return c

            def fill_chunk(j, c):
                chunk_copy(mid + j * ZERO_CHUNK).start()
                return c

            chunks = (hi - mid) >> ZERO_SHIFT
            lax.fori_loop(lo, mid, fill_row, 0)
            lax.fori_loop(0, chunks, fill_chunk, 0)
            return n_single + (mid - lo), n_chunk + chunks

        n_single, n_chunk = lax.fori_loop(0, n_exp, per_expert, (jnp.int32(0), jnp.int32(0)))

        def drain_row(r, c):
            _row_copy(zrow, 0, xs_ref, 0, zsem, ns).wait()
            return c

        def drain_chunk(j, c):
            chunk_copy(0).wait()
            return c

        lax.fori_loop(0, n_single, drain_row, 0)
        lax.fori_loop(0, n_chunk, drain_chunk, 0)

    for t in range(td):
        for k in range(TOP_K):
            _row_copy(h_ref, t, xs_ref, pos_ref[k, t], sem, ns).start(priority=k % 2)

    hk = _load_chunk_bf16(h_ref, 0, td, ns)
    h1 = jnp.dot(hk, w1_ref[0:PAIR, :], preferred_element_type=F32)
    h3 = jnp.dot(hk, w3_ref[0:PAIR, :], preferred_element_type=F32)
    for s in range(1, ns):
        hk = _load_chunk_bf16(h_ref, s, td, ns)
        h1 = h1 + jnp.dot(hk, w1_ref[s * PAIR:(s + 1) * PAIR, :], preferred_element_type=F32)
        h3 = h3 + jnp.dot(hk, w3_ref[s * PAIR:(s + 1) * PAIR, :], preferred_element_type=F32)
    act = (h1 * jax.nn.sigmoid(h1) * h3).astype(BF16)
    sh_ref[...] = jnp.dot(act, w2_ref[...], preferred_element_type=F32)

    for k in range(TOP_K):
        pltpu.make_async_copy(h_ref, xs_ref.at[pl.ds(0, td * ns)], sem).wait()


def _dispatch(pstart, counts, pos_t, h2, w1s, w3s, w2s, n_rows, ns):
    tokens = h2.shape[0] // ns
    n_exp = counts.shape[0]
    d, de = w1s.shape
    td = min(512, tokens)
    kern = functools.partial(_dispatch_kernel, td=td, n_exp=n_exp, ns=ns)
    const2 = lambda i, *_: (0, 0)
    return pl.pallas_call(
        kern,
        out_shape=(jax.ShapeDtypeStruct((n_rows * ns, LANES), U32),
                   jax.ShapeDtypeStruct((tokens, d), F32)),
        grid_spec=pltpu.PrefetchScalarGridSpec(
            num_scalar_prefetch=2,
            grid=(tokens // td,),
            in_specs=[pl.BlockSpec((TOP_K, td), lambda i, *_: (0, i), memory_space=pltpu.SMEM),
                      pl.BlockSpec((td * ns, LANES), lambda i, *_: (i, 0)),
                      _resident((d, de), const2),
                      _resident((d, de), const2),
                      _resident((de, d), const2)],
            out_specs=(pl.BlockSpec(memory_space=pl.ANY),
                       pl.BlockSpec((td, d), lambda i, *_: (i, 0))),
            scratch_shapes=[pltpu.VMEM((ZERO_CHUNK * ns, LANES), U32),
                            pltpu.SemaphoreType.DMA, pltpu.SemaphoreType.DMA]),
        compiler_params=_cparams(("arbitrary",)),
        name="dispatch",
    )(pstart, counts, pos_t, h2, w1s, w3s, w2s)


def _experts_kernel(bexp_ref, nused_ref, x_ref, w1_ref, w3_ref, w2_ref, y_ref,
                    wb1, wb3, wb2, xb):
    b = pl.program_id(0)
    live = b < nused_ref[0]
    changed = jnp.logical_or(b == 0, bexp_ref[b] != bexp_ref[jnp.maximum(b - 1, 0)])

    @pl.when(jnp.logical_and(live, changed))
    def _():
        wb1[...] = w1_ref[0, 0].astype(BF16)
        wb3[...] = w3_ref[0, 0].astype(BF16)
        wb2[...] = w2_ref[0, 0].astype(BF16)

    @pl.when(live)
    def _():
        ns = wb1.shape[0] // PAIR
        for s in range(ns):
            xb[:, s * PAIR:(s + 1) * PAIR] = _load_chunk_bf16(x_ref, s, EXPERT_BLOCK, ns)
        h1 = jnp.dot(xb[...], wb1[...], preferred_element_type=F32)
        h3 = jnp.dot(xb[...], wb3[...], preferred_element_type=F32)
        act = (h1 * jax.nn.sigmoid(h1) * h3).astype(BF16)
        _store_rows(y_ref, jnp.dot(act, wb2[...], preferred_element_type=F32), EXPERT_BLOCK)


def _experts(block_exp, nused, xs, w1, w3, w2, layer):
    d, de = w1.shape[2], w1.shape[3]
    ns = d // PAIR
    nb = xs.shape[0] // (EXPERT_BLOCK * ns)
    blk = lambda b, be, nu: (jnp.minimum(b, nu[0] - 1), 0)
    wsel = lambda b, be, nu: (layer, be[jnp.minimum(b, nu[0] - 1)], 0, 0)
    return pl.pallas_call(
        _experts_kernel,
        out_shape=jax.ShapeDtypeStruct(xs.shape, U32),
        grid_spec=pltpu.PrefetchScalarGridSpec(
            num_scalar_prefetch=2,
            grid=(nb,),
            in_specs=[pl.BlockSpec((EXPERT_BLOCK * ns, LANES), blk),
                      pl.BlockSpec((1, 1, d, de), wsel),
                      pl.BlockSpec((1, 1, d, de), wsel),
                      pl.BlockSpec((1, 1, de, d), wsel)],
            out_specs=pl.BlockSpec((EXPERT_BLOCK * ns, LANES), blk),
            scratch_shapes=[pltpu.VMEM((d, de), BF16), pltpu.VMEM((d, de), BF16),
                            pltpu.VMEM((de, d), BF16), pltpu.VMEM((EXPERT_BLOCK, d), BF16)]),
        compiler_params=_cparams(("arbitrary",)),
        name="experts",
    )(block_exp, nused, xs, w1, w3, w2)


def _combine_kernel(pos_ref, y_ref, wts_ref, sh_ref, x1_ref, g2_ref, fg_ref, o_ref, buf, sem,
                    *, tc, ns, last_layer):
    pt = tc // COMBINE_PARTS
    for t in range(tc):
        for k in range(TOP_K):
            pltpu.make_async_copy(_row_tile(y_ref, pos_ref[k, t], ns),
                                  _row_tile(buf.at[k], t, ns),
                                  sem.at[t // pt]).start(priority=k % 2)

    g2 = g2_ref[0]
    for part in range(COMBINE_PARTS):
        for k in range(TOP_K):
            pltpu.make_async_copy(y_ref.at[pl.ds(0, pt * ns)], buf.at[k, pl.ds(0, pt * ns)],
                                  sem.at[part]).wait()
        rows = slice(part * pt, (part + 1) * pt)
        w = wts_ref[rows, :]
        for s in range(ns):
            picked = pl.ds(part * pt * ns + s, pt, stride=ns)
            r_lo, r_hi = _unpack_pair(buf.at[0][picked, :])
            r_lo = r_lo * w[:, 0:1]
            r_hi = r_hi * w[:, 0:1]
            for k in range(1, TOP_K):
                y_lo, y_hi = _unpack_pair(buf.at[k][picked, :])
                r_lo = r_lo + y_lo * w[:, k:k + 1]
                r_hi = r_hi + y_hi * w[:, k:k + 1]
            c0 = slice(s * PAIR, s * PAIR + LANES)
            c1 = slice(s * PAIR + LANES, (s + 1) * PAIR)
            o_ref[0, rows, c0] = x1_ref[0, rows, c0] + g2[:, c0] * (r_lo + sh_ref[0, rows, c0])
            o_ref[0, rows, c1] = x1_ref[0, rows, c1] + g2[:, c1] * (r_hi + sh_ref[0, rows, c1])
        if last_layer:
            x = o_ref[0, rows, :]
            ms = jnp.mean(x * x, axis=-1, keepdims=True)
            o_ref[0, rows, :] = (x * lax.rsqrt(ms + EPS)) * fg_ref[...]


def _combine(pos_t, y, wts_tok, shared, x1, g2, final_g, last_layer):
    bsz, seq, d = x1.shape
    ns = d // PAIR
    tc = min(512, seq)
    nt = seq // tc
    kern = functools.partial(_combine_kernel, tc=tc, ns=ns, last_layer=last_layer)
    row = lambda b, i: (b, i, 0)
    return pl.pallas_call(
        kern,
        out_shape=jax.ShapeDtypeStruct((bsz, seq, d), F32),
        grid=(bsz, nt),
        in_specs=[pl.BlockSpec((TOP_K, tc), lambda b, i: (0, b * nt + i),
                               memory_space=pltpu.SMEM),
                  pl.BlockSpec(memory_space=pl.ANY),
                  pl.BlockSpec((tc, TOP_K), lambda b, i: (b * nt + i, 0)),
                  pl.BlockSpec((1, tc, d), row),
                  pl.BlockSpec((1, tc, d), row),
                  pl.BlockSpec((1, 1, d), lambda b, i: (b, 0, 0)),
                  pl.BlockSpec((1, d), lambda b, i: (0, 0))],
        out_specs=pl.BlockSpec((1, tc, d), row),
        scratch_shapes=[pltpu.VMEM((TOP_K, tc * ns, LANES), U32),
                        pltpu.SemaphoreType.DMA((COMBINE_PARTS,))],
        compiler_params=_cparams(("arbitrary", "arbitrary")),
        name="combine",
    )(pos_t, y, wts_tok, shared, x1, g2, final_g)


def _pad_lanes(w, n):
    return jnp.pad(w, ((0, 0), (0, n - w.shape[1])))


def kernel(x, c, norm1_g, norm2_g, w_ada, b_ada, w_in, b_f, w_pool, pool_scale, w_o,
           w_router, router_bias, w1, w3, w2, w1_shared, w3_shared, w2_shared, final_g):
    bsz, seq, d = x.shape
    depth = w_ada.shape[0]
    n_heads = b_f.shape[1]
    da = n_heads * HEAD_DIM
    dp = d - da
    n_exp = w_router.shape[2]
    tokens = bsz * seq
    n_assign = tokens * TOP_K
    n_blocks = -(-n_assign // EXPERT_BLOCK) + n_exp
    n_rows = n_blocks * EXPERT_BLOCK

    mod = _ada(c, w_ada, b_ada)
    mod = mod.reshape(depth, bsz, N_MOD, 1, d)

    for l in range(depth):
        sh1, sc1, g1, sh2, sc2, g2 = (mod[l, :, j] for j in range(N_MOD))
        wl = w_in[l]
        wq = wl[:, :da].astype(BF16)
        wk = wl[:, da:2 * da].astype(BF16)
        wv = wl[:, 2 * da:3 * da].astype(BF16)
        wf = _pad_lanes(wl[:, 3 * da:3 * da + n_heads], LANES).astype(BF16)
        wu = wl[:, 3 * da + n_heads:].astype(BF16)
        bf = _pad_lanes(b_f[l][None, :], LANES)
        q, k, v, dcol, drow, p = _proj(
            x, norm1_g[l][None, :], sh1, sc1, wq, wk, wv, wf, bf, wu,
            w_pool[l].astype(BF16), pool_scale[l][None, :])
        a = _attn(q, k, v, dcol, drow, n_heads=n_heads)

        wr = _pad_lanes(w_router[l], LANES)
        wr_hi = wr.astype(BF16)
        wr_lo = (wr - wr_hi.astype(F32)).astype(BF16)
        wo = w_o[l].astype(BF16)
        x1, h2, idx_t, wts_t, rank_t, counts = _post(
            a, p, x, g1, norm2_g[l][None, :], sh2, sc2, wo[:da], wo[da:],
            jnp.concatenate([wr_hi, wr_lo], axis=1), wr_hi, router_bias[l][:, None])

        counts = counts[:, 0]
        padded = ((counts + EXPERT_BLOCK - 1) // EXPERT_BLOCK) * EXPERT_BLOCK
        cum = jnp.cumsum(padded)
        pstart = jnp.concatenate([jnp.zeros((1,), I32), cum]).astype(I32)
        nused = (cum[-1:] // EXPERT_BLOCK).astype(I32)
        block_starts = jnp.arange(n_blocks, dtype=I32) * EXPERT_BLOCK
        block_exp = jnp.minimum(
            jnp.sum((cum[None, :] <= block_starts[:, None]).astype(I32), axis=1), n_exp - 1)

        pos_t = _positions(pstart, idx_t, rank_t, n_exp)
        xs, shared = _dispatch(pstart, counts, pos_t, h2, w1_shared[l].astype(BF16),
                               w3_shared[l].astype(BF16), w2_shared[l].astype(BF16),
                               n_rows, d // PAIR)
        y = _experts(block_exp, nused, xs, w1, w3, w2, l)
        x = _combine(pos_t, y, wts_t.T, shared.reshape(bsz, seq, d), x1, g2,
                     final_g[None, :], l == depth - 1)
    return x
```
